```python
import math
import jax, jax.numpy as jnp
from jax import lax
import numpy as np

D_MODEL = 1024
BATCH = 1
SEQ = 16384
DEPTH = 2
DEC_BATCH = 32
DEC_SEQ = 1
PAST_LEN = 16384
PAGE_SIZE = 128

N_MIXERS = 2
N_RET_LAYERS = (DEPTH + 1) // 2
N_ATT_LAYERS = DEPTH // 2
RET_HEADS = 8
RET_DK = D_MODEL // RET_HEADS
RET_DV = 2 * D_MODEL // RET_HEADS
RET_CHUNK = 128
ATT_HEADS = 8
ATT_DH = D_MODEL // (2 * ATT_HEADS)
ATT_KDIM = 2 * ATT_DH
ATT_DV = 2 * ATT_DH
Q_BLOCK = 128
D_FF = ((8 * D_MODEL // 3 + 127) // 128) * 128
NORM_EPS = 1e-6
GN_EPS = 1e-5
NEG_INF = -1e30

kernel_name = 'retention_diffattn_macaron_hybrid_step'


def _page_counts():
    n_pages = PAST_LEN // PAGE_SIZE
    used = DEC_BATCH * n_pages
    n_pool = used + max(1, used // 4)
    return n_pages, n_pool


def rms_norm(x, g):
    xf = x.astype(jnp.float32)
    y = xf * lax.rsqrt(jnp.mean(xf * xf, axis=-1, keepdims=True) + NORM_EPS)
    return (y * g.astype(jnp.float32)).astype(x.dtype)


def swiglu_ffn(x, w_gu, w_d):
    gate, up = jnp.split(x @ w_gu, 2, axis=-1)
    return (jax.nn.silu(gate) * up) @ w_d


def ret_log_gamma():
    return jnp.log1p(-jnp.exp2(-5.0 - jnp.arange(RET_HEADS, dtype=jnp.float32)))


def alibi_slopes():
    return jnp.exp2(-8.0 * (jnp.arange(ATT_HEADS, dtype=jnp.float32) + 1.0) / ATT_HEADS)


def retention_chunk(s, qkv):
    q, k, v = qkv
    L = q.shape[1]
    lg = ret_log_gamma()
    pos = jnp.arange(L, dtype=jnp.float32)
    n = pos[:, None] - pos[None, :]
    decay = jnp.where(n[None] >= 0, jnp.exp(lg[:, None, None] * jnp.maximum(n, 0.0)[None]), 0.0)
    scores = jnp.einsum('bihd,bjhd->bhij', q, k) * decay[None]
    o = jnp.einsum('bhij,bjhe->bihe', scores, v)
    q_dec = jnp.exp(lg[None, :] * (pos[:, None] + 1.0))
    o = o + jnp.einsum('bihd,bhde->bihe', q * q_dec[None, :, :, None], s)
    k_dec = jnp.exp(lg[None, :] * (L - 1.0 - pos)[:, None])
    s_new = jnp.exp(lg * L)[None, :, None, None] * s + jnp.einsum('bjhd,bjhe->bhde', k * k_dec[None, :, :, None], v)
    return s_new, o


def retention_mix(h, s0, w_in, w_out, gn_g, gn_b):
    B, T, _ = h.shape
    HK, HV = RET_HEADS * RET_DK, RET_HEADS * RET_DV
    q, k, v, g = jnp.split(h @ w_in, [HK, 2 * HK, 2 * HK + HV], axis=-1)
    q = q.reshape(B, T, RET_HEADS, RET_DK).astype(jnp.float32)
    k = k.reshape(B, T, RET_HEADS, RET_DK).astype(jnp.float32) * (RET_DK ** -0.5)
    v = v.reshape(B, T, RET_HEADS, RET_DV).astype(jnp.float32)
    C = RET_CHUNK if T % RET_CHUNK == 0 else T
    nc = T // C

    def chunks(a):
        return a.reshape(B, nc, C, *a.shape[2:]).swapaxes(0, 1)

    s_fin, o = lax.scan(retention_chunk, s0.astype(jnp.float32), (chunks(q), chunks(k), chunks(v)))
    o = o.swapaxes(0, 1).reshape(B, T, RET_HEADS, RET_DV)
    mu = jnp.mean(o, axis=-1, keepdims=True)
    var = jnp.mean(jnp.square(o - mu), axis=-1, keepdims=True)
    o = ((o - mu) * lax.rsqrt(var + GN_EPS)).reshape(B, T, HV)
    o = o * gn_g.astype(jnp.float32) + gn_b.astype(jnp.float32)
    o = (o * jax.nn.silu(g.astype(jnp.float32))).astype(h.dtype)
    return o @ w_out, s_fin.astype(h.dtype)


def diff_attend(q, q_pos, segs, lam, slopes):
    logits = []
    for k, _, k_pos in segs:
        l = jnp.einsum('bqhcd,bkhcd->bhcqk', q, k).astype(jnp.float32)
        dist = (q_pos[:, None] - k_pos[None, :]).astype(jnp.float32)
        l = l - slopes[:, None, None, None] * dist
        logits.append(jnp.where(dist >= 0, l, NEG_INF))
    p = jax.nn.softmax(jnp.concatenate(logits, axis=-1), axis=-1)
    a = p[:, :, 0] - lam * p[:, :, 1]
    out = None
    off = 0
    for k, v, _ in segs:
        Lk = k.shape[1]
        term = jnp.einsum('bhqk,bkhe->bqhe', a[..., off:off + Lk].astype(v.dtype), v)
        out = term if out is None else out + term
        off += Lk
    return out


def diff_attn_mix(h, layer_idx, w_in, w_out, lq1, lk1, lq2, lk2, sub_g, past):
    B, T, _ = h.shape
    q, k, v = jnp.split(h @ w_in, 3, axis=-1)
    q = q.reshape(B, T, ATT_HEADS, 2, ATT_DH) * (ATT_DH ** -0.5)
    k = k.reshape(B, T, ATT_HEADS, 2, ATT_DH)
    v = v.reshape(B, T, ATT_HEADS, ATT_DV)
    lam_init = 0.8 - 0.6 * math.exp(-0.3 * layer_idx)
    lam = (jnp.exp(jnp.sum(lq1.astype(jnp.float32) * lk1.astype(jnp.float32)))
           - jnp.exp(jnp.sum(lq2.astype(jnp.float32) * lk2.astype(jnp.float32))) + lam_init)
    slopes = alibi_slopes()
    if past is None:
        pos = jnp.arange(T, dtype=jnp.int32)
        QB = Q_BLOCK if T % Q_BLOCK == 0 else T
        nb = T // QB
        qb = q.reshape(B, nb, QB, ATT_HEADS, 2, ATT_DH).swapaxes(0, 1)
        pb = pos.reshape(nb, QB)
        o = lax.map(lambda a: diff_attend(a[0], a[1], [(k, v, pos)], lam, slopes), (qb, pb))
        o = o.swapaxes(0, 1).reshape(B, T, ATT_HEADS, ATT_DV)
    else:
        k_past, v_past = past
        P = k_past.shape[1]
        past_pos = jnp.arange(P, dtype=jnp.int32)
        pos = P + jnp.arange(T, dtype=jnp.int32)
        o = diff_attend(q, pos, [(k_past, v_past, past_pos), (k, v, pos)], lam, slopes)
    o = rms_norm(o, sub_g) * (1.0 - lam_init)
    out = o.reshape(B, T, ATT_HEADS * ATT_DV).astype(h.dtype) @ w_out
    return out, k.reshape(B, T, ATT_HEADS, ATT_KDIM), v


def setup_inputs(seed: int = 0) -> dict:
    key = jax.random.key(seed)
    ks = jax.random.split(key, 24)
    f32 = jnp.float32
    n_pages, n_pool = _page_counts()

    def w(k, shape, fan_in):
        return jax.random.normal(k, shape, f32) * (fan_in ** -0.5)

    x_prompt = jax.random.normal(ks[0], (BATCH, SEQ, D_MODEL), f32)
    x_sample = jax.random.normal(ks[1], (DEC_BATCH, DEC_SEQ, D_MODEL), f32)
    state_ret = jax.random.normal(ks[2], (N_RET_LAYERS, DEC_BATCH, RET_HEADS, RET_DK, RET_DV), f32) * (RET_DK ** -0.5)
    cache_k = jax.random.normal(ks[3], (N_ATT_LAYERS, n_pool, PAGE_SIZE, ATT_HEADS, ATT_KDIM), f32)
    cache_v = jax.random.normal(ks[4], (N_ATT_LAYERS, n_pool, PAGE_SIZE, ATT_HEADS, ATT_DV), f32)
    page_table = jax.random.permutation(ks[5], n_pool)[:DEC_BATCH * n_pages].reshape(DEC_BATCH, n_pages).astype(jnp.int32)
    norm_g = 1.0 + 0.02 * jax.random.normal(ks[6], (DEPTH, 3, D_MODEL), f32)
    ffn_w_gu = w(ks[7], (DEPTH, 2, D_MODEL, 2 * D_FF), D_MODEL)
    ffn_w_d = w(ks[8], (DEPTH, 2, D_FF, D_MODEL), D_FF)
    final_g = 1.0 + 0.02 * jax.random.normal(ks[9], (D_MODEL,), f32)
    ret_in_width = 2 * RET_HEADS * RET_DK + 2 * RET_HEADS * RET_DV
    ret_w_in = w(ks[10], (N_RET_LAYERS, D_MODEL, ret_in_width), D_MODEL)
    ret_w_out = w(ks[11], (N_RET_LAYERS, RET_HEADS * RET_DV, D_MODEL), RET_HEADS * RET_DV)
    ret_gn_g = 1.0 + 0.02 * jax.random.normal(ks[12], (N_RET_LAYERS, RET_HEADS * RET_DV), f32)
    ret_gn_b = 0.02 * jax.random.normal(ks[13], (N_RET_LAYERS, RET_HEADS * RET_DV), f32)
    att_w_in = w(ks[14], (N_ATT_LAYERS, D_MODEL, 3 * D_MODEL), D_MODEL)
    att_w_out = w(ks[15], (N_ATT_LAYERS, ATT_HEADS * ATT_DV, D_MODEL), ATT_HEADS * ATT_DV)
    att_lam_q1 = 0.1 * jax.random.normal(ks[16], (N_ATT_LAYERS, ATT_DH), f32)
    att_lam_k1 = 0.1 * jax.random.normal(ks[17], (N_ATT_LAYERS, ATT_DH), f32)
    att_lam_q2 = 0.1 * jax.random.normal(ks[18], (N_ATT_LAYERS, ATT_DH), f32)
    att_lam_k2 = 0.1 * jax.random.normal(ks[19], (N_ATT_LAYERS, ATT_DH), f32)
    att_subln_g = 1.0 + 0.02 * jax.random.normal(ks[20], (N_ATT_LAYERS, ATT_DV), f32)
    return {'x_prompt': x_prompt, 'x_sample': x_sample, 'state_ret': state_ret,
            'cache_k': cache_k, 'cache_v': cache_v, 'page_table': page_table,
            'norm_g': norm_g, 'ffn_w_gu': ffn_w_gu, 'ffn_w_d': ffn_w_d, 'final_g': final_g,
            'ret_w_in': ret_w_in, 'ret_w_out': ret_w_out, 'ret_gn_g': ret_gn_g, 'ret_gn_b': ret_gn_b,
            'att_w_in': att_w_in, 'att_w_out': att_w_out, 'att_lam_q1': att_lam_q1,
            'att_lam_k1': att_lam_k1, 'att_lam_q2': att_lam_q2, 'att_lam_k2': att_lam_k2,
            'att_subln_g': att_subln_g}


def reference(x_prompt, x_sample, state_ret, cache_k, cache_v, page_table,
              norm_g, ffn_w_gu, ffn_w_d, final_g,
              ret_w_in, ret_w_out, ret_gn_g, ret_gn_b,
              att_w_in, att_w_out, att_lam_q1, att_lam_k1, att_lam_q2, att_lam_k2, att_subln_g):
    yp, ys = x_prompt, x_sample
    ret_p, ret_s, kp_rows, vp_rows, ks_rows, vs_rows = [], [], [], [], [], []
    for i in range(DEPTH):
        r = i // N_MIXERS
        yp = yp + 0.5 * swiglu_ffn(rms_norm(yp, norm_g[i, 0]), ffn_w_gu[i, 0], ffn_w_d[i, 0])
        ys = ys + 0.5 * swiglu_ffn(rms_norm(ys, norm_g[i, 0]), ffn_w_gu[i, 0], ffn_w_d[i, 0])
        hp = rms_norm(yp, norm_g[i, 1])
        hs = rms_norm(ys, norm_g[i, 1])
        if i % N_MIXERS == 0:
            s0 = jnp.zeros((BATCH, RET_HEADS, RET_DK, RET_DV), jnp.float32)
            op, sp = retention_mix(hp, s0, ret_w_in[r], ret_w_out[r], ret_gn_g[r], ret_gn_b[r])
            osm, ss = retention_mix(hs, state_ret[r], ret_w_in[r], ret_w_out[r], ret_gn_g[r], ret_gn_b[r])
            ret_p.append(sp)
            ret_s.append(ss)
        else:
            op, kp, vp = diff_attn_mix(hp, i, att_w_in[r], att_w_out[r], att_lam_q1[r], att_lam_k1[r],
                                       att_lam_q2[r], att_lam_k2[r], att_subln_g[r], None)
            k_past = cache_k[r, page_table].reshape(DEC_BATCH, -1, ATT_HEADS, 2, ATT_DH)
            v_past = cache_v[r, page_table].reshape(DEC_BATCH, -1, ATT_HEADS, ATT_DV)
            osm, ksm, vsm = diff_attn_mix(hs, i, att_w_in[r], att_w_out[r], att_lam_q1[r], att_lam_k1[r],
                                          att_lam_q2[r], att_lam_k2[r], att_subln_g[r], (k_past, v_past))
            kp_rows.append(kp)
            vp_rows.append(vp)
            ks_rows.append(ksm)
            vs_rows.append(vsm)
        yp = yp + op
        ys = ys + osm
        yp = yp + 0.5 * swiglu_ffn(rms_norm(yp, norm_g[i, 2]), ffn_w_gu[i, 1], ffn_w_d[i, 1])
        ys = ys + 0.5 * swiglu_ffn(rms_norm(ys, norm_g[i, 2]), ffn_w_gu[i, 1], ffn_w_d[i, 1])
    y_prompt = rms_norm(yp, final_g)
    y_sample = rms_norm(ys, final_g)
    state_ret_prompt = jnp.stack(ret_p)
    k_prompt = jnp.stack(kp_rows)
    v_prompt = jnp.stack(vp_rows)
    state_ret_sample = jnp.stack(ret_s)
    k_sample = jnp.stack(ks_rows)
    v_sample = jnp.stack(vs_rows)
    return (y_prompt, y_sample, state_ret_prompt, k_prompt, v_prompt, state_ret_sample, k_sample, v_sample)
```

```python
import functools
import math

import numpy as np
import jax
import jax.numpy as jnp
from jax import lax
from jax.experimental import pallas as pl
from jax.experimental.pallas import tpu as pltpu

F32 = jnp.float32
BF16 = jnp.bfloat16

NORM_EPS = 1e-6
GN_EPS = 1e-5
NEG_BIG = -1e30
N_MIXERS = 2

RET_HEADS = 8
RET_CHUNK = 128
ATT_HEADS = 8

V7X_VMEM_BYTES = 64 * 1024 * 1024
MXU_TILE = 256
LANES = 128


def _vmem_limit(nbytes):
    return int(min(V7X_VMEM_BYTES - 8 * 1024 * 1024, max(32 * 1024 * 1024, nbytes)))


def _rms(x, g):
    return x * lax.rsqrt(jnp.mean(x * x, axis=-1, keepdims=True) + NORM_EPS) * g


def _resident(shape):
    zeros = (0,) * len(shape)
    return pl.BlockSpec(shape, lambda *_: zeros, pipeline_mode=pl.Buffered(1))


def _ffn_kernel(x_ref, g_ref, wgu_ref, wd_ref, fg_ref, o_ref, xn_ref, acc_ref, *, n_chunks, ck, final):
    x = x_ref[...]
    xn_ref[...] = _rms(x, g_ref[...]).astype(BF16)
    acc_ref[...] = jnp.zeros_like(acc_ref)

    def chunk(c, carry):
        gu = jnp.dot(xn_ref[...], wgu_ref[c], preferred_element_type=F32)
        gate = gu[:, :ck]
        up = gu[:, ck:]
        h = (jax.nn.silu(gate) * up).astype(BF16)
        acc_ref[...] += jnp.dot(h, wd_ref[c], preferred_element_type=F32)
        return carry

    lax.fori_loop(0, n_chunks, chunk, 0)
    y = x + 0.5 * acc_ref[...]
    if final:
        y = _rms(y, fg_ref[...])
    o_ref[...] = y


def _ffn(x, g, wgu_r, wd_r, final_g, *, tm, final):
    m, d = x.shape
    n_chunks, _, ck2 = wgu_r.shape
    ck = ck2 // 2
    tm = min(tm, m)
    est = (wgu_r.size + wd_r.size) * 2 + tm * d * (4 * 4 + 2 + 4) + tm * ck2 * 4 * 3
    return pl.pallas_call(
        functools.partial(_ffn_kernel, n_chunks=n_chunks, ck=ck, final=final),
        out_shape=jax.ShapeDtypeStruct((m, d), F32),
        grid=(m // tm,),
        in_specs=[
            pl.BlockSpec((tm, d), lambda i: (i, 0)),
            _resident((1, d)),
            _resident(wgu_r.shape),
            _resident(wd_r.shape),
            _resident((1, d)),
        ],
        out_specs=pl.BlockSpec((tm, d), lambda i: (i, 0)),
        scratch_shapes=[pltpu.VMEM((tm, d), BF16), pltpu.VMEM((tm, d), F32)],
        compiler_params=pltpu.CompilerParams(
            dimension_semantics=("arbitrary",), vmem_limit_bytes=_vmem_limit(est + (8 << 20))),
        name="ffn",
    )(x, g.reshape(1, d), wgu_r, wd_r, final_g.reshape(1, d))


def _ret_proj_kernel(x_ref, g_ref, w_ref, o_ref, *, d, k_scale):
    xn = _rms(x_ref[...], g_ref[...]).astype(BF16)
    n = w_ref.shape[1]
    for c in range(n // d):
        r = jnp.dot(xn, w_ref[:, c * d:(c + 1) * d], preferred_element_type=F32)
        if c == 1:
            r = r * k_scale
        o_ref[:, c * d:(c + 1) * d] = r.astype(o_ref.dtype)


def _ret_proj(x, g, w, *, tm, out_dtype, k_scale):
    m, d = x.shape
    n = w.shape[1]
    tm = min(tm, m)
    est = w.size * 2 + tm * d * 4 * 2 + tm * n * 4 * 2 + tm * d * 4 * 3
    return pl.pallas_call(
        functools.partial(_ret_proj_kernel, d=d, k_scale=k_scale),
        out_shape=jax.ShapeDtypeStruct((m, n), out_dtype),
        grid=(m // tm,),
        in_specs=[pl.BlockSpec((tm, d), lambda i: (i, 0)), _resident((1, d)), _resident(w.shape)],
        out_specs=pl.BlockSpec((tm, n), lambda i: (i, 0)),
        compiler_params=pltpu.CompilerParams(
            dimension_semantics=("arbitrary",), vmem_limit_bytes=_vmem_limit(est + (8 << 20))),
        name="ret_proj",
    )(x, g.reshape(1, d), w)


def _out_proj_kernel(y_ref, o_ref, w_ref, out_ref):
    out_ref[...] = y_ref[...] + jnp.dot(o_ref[...].astype(BF16), w_ref[...], preferred_element_type=F32)


def _out_proj(y, o, w, *, tm):
    m, d = y.shape
    kdim = o.shape[1]
    tm = min(tm, m)
    est = w.size * 2 + tm * d * 4 * 4 + tm * kdim * 4 * 2 + tm * d * 4
    return pl.pallas_call(
        _out_proj_kernel,
        out_shape=jax.ShapeDtypeStruct((m, d), F32),
        grid=(m // tm,),
        in_specs=[pl.BlockSpec((tm, d), lambda i: (i, 0)),
                  pl.BlockSpec((tm, kdim), lambda i: (i, 0)),
                  _resident(w.shape)],
        out_specs=pl.BlockSpec((tm, d), lambda i: (i, 0)),
        compiler_params=pltpu.CompilerParams(
            dimension_semantics=("arbitrary",), vmem_limit_bytes=_vmem_limit(est + (8 << 20))),
        name="out_proj",
    )(y, o, w)


def _ret_consts(chunk):
    lg = np.log1p(-np.exp2(-5.0 - np.arange(RET_HEADS, dtype=np.float64)))
    pos = np.arange(chunk, dtype=np.float64)
    n = pos[:, None] - pos[None, :]
    decay = np.where(n[None] >= 0, np.exp(lg[:, None, None] * np.maximum(n, 0.0)[None]), 0.0)
    q_dec = np.exp(lg[:, None] * (pos[None, :] + 1.0))
    k_dec = np.exp(lg[:, None] * (chunk - 1.0 - pos)[None, :])
    s_dec = np.exp(lg * chunk)
    return decay, q_dec, k_dec, s_dec


def _group_norm_gate(o, gate, gn_g, gn_b):
    mu = jnp.mean(o, axis=-1, keepdims=True)
    var = jnp.mean(jnp.square(o - mu), axis=-1, keepdims=True)
    on = (o - mu) * lax.rsqrt(var + GN_EPS)
    on = on * gn_g + gn_b
    return on * jax.nn.silu(gate)


def _ret_scan_kernel(q_ref, k_ref, v_ref, gt_ref, dec_ref, qd_ref, kd_ref, gng_ref, gnb_ref,
                     o_ref, s_out_ref, s_ref, *, dk, dv, s_dec):
    t = pl.program_id(0)

    @pl.when(t == 0)
    def _():
        s_ref[...] = jnp.zeros_like(s_ref)

    for h in range(RET_HEADS):
        qh = q_ref[:, h * dk:(h + 1) * dk]
        kh = k_ref[:, h * dk:(h + 1) * dk]
        vh = v_ref[:, h * dv:(h + 1) * dv]
        sc = lax.dot_general(qh, kh, (((1,), (1,)), ((), ())), preferred_element_type=F32) * dec_ref[h]
        o = jnp.dot(sc.astype(BF16), vh, preferred_element_type=F32)
        s_old = s_ref[h]
        qd = (qh.astype(F32) * qd_ref[h]).astype(BF16)
        o = o + jnp.dot(qd, s_old.astype(BF16), preferred_element_type=F32)
        kd = (kh.astype(F32) * kd_ref[h]).astype(BF16)
        s_ref[h] = s_dec[h] * s_old + lax.dot_general(kd, vh, (((0,), (0,)), ((), ())),
                                                      preferred_element_type=F32)
        gate = gt_ref[:, h * dv:(h + 1) * dv].astype(F32)
        res = _group_norm_gate(o, gate, gng_ref[:, h * dv:(h + 1) * dv], gnb_ref[:, h * dv:(h + 1) * dv])
        o_ref[:, h * dv:(h + 1) * dv] = res.astype(o_ref.dtype)

    @pl.when(t == pl.num_programs(0) - 1)
    def _():
        s_out_ref[...] = s_ref[...]


def _ret_scan(qkvg, gn_g, gn_b, *, dk, dv):
    t_len = qkvg.shape[0]
    c = RET_CHUNK if t_len % RET_CHUNK == 0 else t_len
    hk, hv = RET_HEADS * dk, RET_HEADS * dv
    decay, q_dec, k_dec, s_dec = _ret_consts(c)
    dec = jnp.asarray(decay, F32)
    qd = jnp.asarray(np.broadcast_to(q_dec[:, :, None], (RET_HEADS, c, dk)), F32)
    kd = jnp.asarray(np.broadcast_to(k_dec[:, :, None], (RET_HEADS, c, dk)), F32)
    kq = hv // hk
    return pl.pallas_call(
        functools.partial(_ret_scan_kernel, dk=dk, dv=dv, s_dec=tuple(float(s) for s in s_dec)),
        out_shape=(jax.ShapeDtypeStruct((t_len, hv), BF16),
                   jax.ShapeDtypeStruct((RET_HEADS, dk, dv), F32)),
        grid=(t_len // c,),
        in_specs=[
            pl.BlockSpec((c, hk), lambda t: (t, 0)),
            pl.BlockSpec((c, hk), lambda t: (t, 1)),
            pl.BlockSpec((c, hv), lambda t: (t, 1)),
            pl.BlockSpec((c, hv), lambda t: (t, 2)),
            _resident(dec.shape), _resident(qd.shape), _resident(kd.shape),
            _resident((1, hv)), _resident((1, hv)),
        ],
        out_specs=(pl.BlockSpec((c, hv), lambda t: (t, 0)),
                   pl.BlockSpec((RET_HEADS, dk, dv), lambda t: (0, 0, 0))),
        scratch_shapes=[pltpu.VMEM((RET_HEADS, dk, dv), F32)],
        compiler_params=pltpu.CompilerParams(dimension_semantics=("arbitrary",)),
        name="ret_scan",
    )(qkvg, qkvg, qkvg, qkvg, dec, qd, kd, gn_g.reshape(1, hv), gn_b.reshape(1, hv))


def _ret_step_kernel(x_ref, s_ref, gng_ref, gnb_ref, o_ref, s_out_ref, *, dk, dv, gamma):
    hk = RET_HEADS * dk
    hv = RET_HEADS * dv
    for h in range(RET_HEADS):
        q = x_ref[:, h * dk:(h + 1) * dk]
        k = x_ref[:, hk + h * dk:hk + (h + 1) * dk]
        v = x_ref[:, 2 * hk + h * dv:2 * hk + (h + 1) * dv]
        gate = x_ref[:, 2 * hk + hv + h * dv:2 * hk + hv + (h + 1) * dv]
        q_col = jnp.transpose(jnp.broadcast_to(q, (dk, dk)))[:, :1]
        k_col = jnp.transpose(jnp.broadcast_to(k, (dk, dk)))[:, :1]
        s_old = s_ref[h]
        qk = jnp.sum(q * k, axis=1, keepdims=True)
        o = qk * v + jnp.sum((q_col * gamma[h]) * s_old, axis=0, keepdims=True)
        s_out_ref[h] = gamma[h] * s_old + k_col * v
        res = _group_norm_gate(o, gate, gng_ref[:, h * dv:(h + 1) * dv], gnb_ref[:, h * dv:(h + 1) * dv])
        o_ref[:, h * dv:(h + 1) * dv] = res


def _ret_step(qkvg, state, layer, gn_g, gn_b, *, dk, dv):
    b, n = qkvg.shape
    hv = RET_HEADS * dv
    gamma = tuple(float(1.0 - 2.0 ** (-5.0 - h)) for h in range(RET_HEADS))
    o, s_new = pl.pallas_call(
        functools.partial(_ret_step_kernel, dk=dk, dv=dv, gamma=gamma),
        out_shape=(jax.ShapeDtypeStruct((b, 1, hv), F32),
                   jax.ShapeDtypeStruct(state.shape[1:], F32)),
        grid=(b,),
        in_specs=[
            pl.BlockSpec((None, 1, n), lambda i: (i, 0, 0)),
            pl.BlockSpec((None, None, RET_HEADS, dk, dv), lambda i: (layer, i, 0, 0, 0)),
            _resident((1, hv)), _resident((1, hv)),
        ],
        out_specs=(pl.BlockSpec((None, 1, hv), lambda i: (i, 0, 0)),
                   pl.BlockSpec((None, RET_HEADS, dk, dv), lambda i: (i, 0, 0, 0))),
        compiler_params=pltpu.CompilerParams(dimension_semantics=("arbitrary",)),
        name="ret_step",
    )(qkvg.reshape(b, 1, n), state, gn_g.reshape(1, hv), gn_b.reshape(1, hv))
    return o.reshape(b, hv), s_new


def _att_proj_kernel(x_ref, g_ref, w_ref, *out_refs, d, dh2, q_scale, head_major):
    xn = _rms(x_ref[...], g_ref[...]).astype(BF16)
    q = jnp.dot(xn, w_ref[:, 0:d], preferred_element_type=F32) * q_scale
    k = jnp.dot(xn, w_ref[:, d:2 * d], preferred_element_type=F32)
    v = jnp.dot(xn, w_ref[:, 2 * d:3 * d], preferred_element_type=F32)
    if head_major:
        k_ref, v_ref, qb_ref, kb_ref, vb_ref = out_refs
        k_ref[...] = k
        v_ref[...] = v
        for h in range(ATT_HEADS):
            qb_ref[h] = q[:, h * dh2:(h + 1) * dh2].astype(BF16)
            kb_ref[h] = k[:, h * dh2:(h + 1) * dh2].astype(BF16)
            vb_ref[h] = v[:, h * dh2:(h + 1) * dh2].astype(BF16)
    else:
        q_ref, k_ref, v_ref = out_refs
        q_ref[...] = q
        k_ref[...] = k
        v_ref[...] = v


def _att_proj(x, g, w, *, tm, head_major):
    m, d = x.shape
    dh2 = d // ATT_HEADS
    tm = min(tm, m)
    row = pl.BlockSpec((tm, d), lambda i: (i, 0))
    if head_major:
        hm = pl.BlockSpec((ATT_HEADS, tm, dh2), lambda i: (0, i, 0))
        out_shape = (jax.ShapeDtypeStruct((m, d), F32),) * 2 + (jax.ShapeDtypeStruct((ATT_HEADS, m, dh2), BF16),) * 3
        out_specs = (row, row, hm, hm, hm)
    else:
        out_shape = (jax.ShapeDtypeStruct((m, d), F32),) * 3
        out_specs = (row, row, row)
    est = w.size * 2 + tm * d * 4 * 12
    return pl.pallas_call(
        functools.partial(_att_proj_kernel, d=d, dh2=dh2, q_scale=(dh2 // 2) ** -0.5, head_major=head_major),
        out_shape=out_shape,
        grid=(m // tm,),
        in_specs=[row, _resident((1, d)), _resident(w.shape)],
        out_specs=out_specs,
        compiler_params=pltpu.CompilerParams(
            dimension_semantics=("arbitrary",), vmem_limit_bytes=_vmem_limit(est + (8 << 20))),
        name="att_proj",
    )(x, g.reshape(1, d), w)


def _alibi_slopes():
    return np.exp2(-8.0 * (np.arange(ATT_HEADS, dtype=np.float64) + 1.0) / ATT_HEADS)


def _lambda(lq1_ref, lk1_ref, lq2_ref, lk2_ref, lam_init):
    a = jnp.sum(lq1_ref[...] * lk1_ref[...], axis=1, keepdims=True)
    b = jnp.sum(lq2_ref[...] * lk2_ref[...], axis=1, keepdims=True)
    return jnp.exp(a) - jnp.exp(b) + lam_init


def _sub_norm(o, sub_g, lam_init):
    return _rms(o, sub_g) * (1.0 - lam_init)


def _flash_kernel(lq1_ref, lk1_ref, lq2_ref, lk2_ref, subg_ref, slope_ref, q_ref, k_ref, v_ref, o_ref,
                  m_ref, acc_ref, *, tq, dh, lam_init):
    i = pl.program_id(1)
    dv = 2 * dh
    q = q_ref[...]
    lane = lax.broadcasted_iota(jnp.int32, q.shape, 1)
    zero = jnp.zeros_like(q)
    qc = (jnp.where(lane < dh, q, zero), jnp.where(lane >= dh, q, zero))
    slope = slope_ref[:, :1]
    col_bias = lax.broadcasted_iota(jnp.int32, (1, tq), 1).astype(F32) * slope
    ones = jnp.ones((tq, dv), BF16)

    m_ref[...] = jnp.full_like(m_ref, NEG_BIG)
    acc_ref[...] = jnp.zeros_like(acc_ref)

    def tile(j, masked):
        start = pl.multiple_of(j * tq, tq)
        kt = k_ref[pl.ds(start, tq), :]
        va = jnp.concatenate([v_ref[pl.ds(start, tq), :], ones], axis=1)
        blk = jnp.full((1, 1), (j - i) * tq, jnp.int32).astype(F32) * slope
        for c in range(2):
            s = lax.dot_general(qc[c], kt, (((1,), (1,)), ((), ())), preferred_element_type=F32) + col_bias
            if masked:
                r = lax.broadcasted_iota(jnp.int32, s.shape, 0)
                cc = lax.broadcasted_iota(jnp.int32, s.shape, 1)
                s = jnp.where(cc <= r, s, NEG_BIG)
            m_old = m_ref[c]
            m_new = jnp.maximum(m_old, jnp.max(s, axis=1, keepdims=True) + blk)
            p = jnp.exp(s - (m_new - blk))
            alpha = jnp.exp(m_old - m_new)
            acc_ref[c] = alpha * acc_ref[c] + jnp.dot(p.astype(BF16), va, preferred_element_type=F32)
            m_ref[c] = m_new

    def body(j, carry):
        tile(j, False)
        return carry

    lax.fori_loop(0, i, body, 0)
    tile(i, True)

    lam = _lambda(lq1_ref, lk1_ref, lq2_ref, lk2_ref, lam_init)
    a1 = acc_ref[0]
    a2 = acc_ref[1]
    o = a1[:, :dv] / a1[:, dv:dv + 1] - lam * (a2[:, :dv] / a2[:, dv:dv + 1])
    o_ref[...] = _sub_norm(o, subg_ref[...], lam_init).astype(o_ref.dtype)


def _flash(qb, kb, vb, lams, sub_g, *, tq, lam_init):
    h, t_len, dh2 = qb.shape
    dh = dh2 // 2
    tq = min(tq, t_len)
    slopes = jnp.asarray(np.broadcast_to(_alibi_slopes()[:, None, None], (h, 1, LANES)), F32)
    lam_specs = [_resident((1, dh))] * 4
    est = 2 * 2 * t_len * dh2 * 2 + tq * tq * 4 * 8 + tq * 4 * dh2 * 4 * 4
    return pl.pallas_call(
        functools.partial(_flash_kernel, tq=tq, dh=dh, lam_init=lam_init),
        out_shape=jax.ShapeDtypeStruct((t_len, h * dh2), BF16),
        grid=(h, t_len // tq),
        in_specs=lam_specs + [
            _resident((1, dh2)),
            pl.BlockSpec((None, 1, LANES), lambda hh, i: (hh, 0, 0)),
            pl.BlockSpec((None, tq, dh2), lambda hh, i: (hh, i, 0)),
            pl.BlockSpec((None, t_len, dh2), lambda hh, i: (hh, 0, 0)),
            pl.BlockSpec((None, t_len, dh2), lambda hh, i: (hh, 0, 0)),
        ],
        out_specs=pl.BlockSpec((tq, dh2), lambda hh, i: (i, hh)),
        scratch_shapes=[pltpu.VMEM((2, tq, 1), F32), pltpu.VMEM((2, tq, 2 * dh2), F32)],
        compiler_params=pltpu.CompilerParams(
            dimension_semantics=("arbitrary", "arbitrary"), vmem_limit_bytes=_vmem_limit(est + (8 << 20))),
        name="flash_diff_attn",
    )(*[a.reshape(1, dh) for a in lams], sub_g.reshape(1, dh2), slopes, qb, kb, vb)


def _dec_attn_kernel(pt_ref, lq1_ref, lk1_ref, lq2_ref, lk2_ref, subg_ref, slope_ref, q_ref, kn_ref, vn_ref,
                     *rest, n_pg, page, past_len, dh, lam_init):
    del pt_ref
    k_refs = rest[:n_pg]
    v_refs = rest[n_pg:2 * n_pg]
    o_ref, m_ref, l_ref, acc_ref = rest[2 * n_pg:]
    g = pl.program_id(1)
    dh2 = 2 * dh
    nh = q_ref.shape[0]

    @pl.when(g == 0)
    def _():
        m_ref[...] = jnp.full_like(m_ref, NEG_BIG)
        l_ref[...] = jnp.zeros_like(l_ref)
        acc_ref[...] = jnp.zeros_like(acc_ref)

    q = q_ref[...]
    slope = slope_ref[...]
    d_idx = lax.broadcasted_iota(jnp.int32, (dh2, 2 * LANES), 0)
    c_idx = lax.broadcasted_iota(jnp.int32, (dh2, 2 * LANES), 1)
    half_sum = jnp.where((d_idx < dh) == (c_idx < LANES), 1.0, 0.0).astype(BF16)
    pos_bias = lax.broadcasted_iota(jnp.int32, (page, nh, LANES), 0).astype(F32) * slope[None]

    def update(s, c, off, v):
        m_old = m_ref[c]
        m_new = jnp.maximum(m_old, jnp.max(s, axis=0) + off)
        p = jnp.exp(s - (m_new - off)[None])
        alpha = jnp.exp(m_old - m_new)
        l_ref[c] = alpha * l_ref[c] + jnp.sum(p, axis=0)
        acc_ref[c] = alpha * acc_ref[c] + jnp.sum(p * v, axis=0)
        m_ref[c] = m_new

    for pi in range(n_pg):
        kp = k_refs[pi][...]
        vp = v_refs[pi][...]
        prod = (kp * q[None]).astype(BF16).reshape(page * nh, dh2)
        sb = jnp.dot(prod, half_sum, preferred_element_type=F32)
        first_pos = (g * n_pg + pi) * page - past_len
        off = jnp.full((1, 1), first_pos, jnp.int32).astype(F32) * slope
        for c in range(2):
            s = sb[:, c * LANES:(c + 1) * LANES].reshape(page, nh, LANES) + pos_bias
            update(s, c, off, vp)

    @pl.when(g == pl.num_programs(1) - 1)
    def _():
        kn = kn_ref[...]
        vn = vn_ref[...]
        ss = jnp.dot((kn * q).astype(BF16), half_sum, preferred_element_type=F32)
        zero_off = jnp.zeros_like(slope)
        for c in range(2):
            update(ss[:, c * LANES:(c + 1) * LANES][None], c, zero_off, vn[None])
        lam = _lambda(lq1_ref, lk1_ref, lq2_ref, lk2_ref, lam_init)
        o = acc_ref[0] / l_ref[0] - lam * (acc_ref[1] / l_ref[1])
        o_ref[...] = _sub_norm(o, subg_ref[...], lam_init)


def _dec_attn(q, kn, vn, cache_k, cache_v, layer, page_table, lams, sub_g, *, lam_init, n_pg):
    b, nh, dh2 = q.shape
    dh = dh2 // 2
    n_pages = page_table.shape[1]
    page = cache_k.shape[2]
    while n_pages % n_pg:
        n_pg //= 2
    slopes = jnp.asarray(np.broadcast_to(_alibi_slopes()[:, None], (nh, LANES)), F32)

    def page_spec(pi):
        return pl.BlockSpec((None, None, page, nh, dh2),
                            lambda bb, g, pt: (layer, pt[bb, g * n_pg + pi], 0, 0, 0))

    tok = pl.BlockSpec((None, nh, dh2), lambda bb, g, pt: (bb, 0, 0))

    def const(shape):
        zeros = (0,) * len(shape)
        return pl.BlockSpec(shape, lambda bb, g, pt: zeros)

    grid_spec = pltpu.PrefetchScalarGridSpec(
        num_scalar_prefetch=1,
        grid=(b, n_pages // n_pg),
        in_specs=[const((1, dh))] * 4 + [const((1, dh2)), const((nh, LANES)), tok, tok, tok]
                 + [page_spec(pi) for pi in range(n_pg)] * 2,
        out_specs=tok,
        scratch_shapes=[pltpu.VMEM((2, nh, LANES), F32), pltpu.VMEM((2, nh, LANES), F32),
                        pltpu.VMEM((2, nh, dh2), F32)],
    )
    est = 2 * 2 * n_pg * page * nh * dh2 * 4 + page * nh * LANES * 4 * 12
    return pl.pallas_call(
        functools.partial(_dec_attn_kernel, n_pg=n_pg, page=page, past_len=n_pages * page, dh=dh,
                          lam_init=lam_init),
        out_shape=jax.ShapeDtypeStruct((b, nh, dh2), F32),
        grid_spec=grid_spec,
        compiler_params=pltpu.CompilerParams(
            dimension_semantics=("arbitrary", "arbitrary"), vmem_limit_bytes=_vmem_limit(est + (8 << 20))),
        name="decode_diff_attn",
    )(page_table, *[a.reshape(1, dh) for a in lams], sub_g.reshape(1, dh2), slopes, q, kn, vn,
      *([cache_k] * n_pg), *([cache_v] * n_pg))


FFN_CHUNK = MXU_TILE
ROW_TILE = 512
FLASH_TILE = 512
DECODE_PAGES_PER_STEP = 4


def _prep_ffn_weights(w_gu, w_d):
    d, two_ff = w_gu.shape
    d_ff = two_ff // 2
    n_chunks = d_ff // FFN_CHUNK
    gate = w_gu[:, :d_ff].reshape(d, n_chunks, 1, FFN_CHUNK)
    up = w_gu[:, d_ff:].reshape(d, n_chunks, 1, FFN_CHUNK)
    wgu_r = jnp.concatenate([gate, up], axis=2).transpose(1, 0, 2, 3).reshape(n_chunks, d, 2 * FFN_CHUNK)
    return wgu_r.astype(BF16), w_d.reshape(n_chunks, FFN_CHUNK, d).astype(BF16)


def kernel(x_prompt, x_sample, state_ret, cache_k, cache_v, page_table, norm_g, ffn_w_gu, ffn_w_d, final_g,
           ret_w_in, ret_w_out, ret_gn_g, ret_gn_b, att_w_in, att_w_out, att_lam_q1, att_lam_k1, att_lam_q2,
           att_lam_k2, att_subln_g):
    depth = norm_g.shape[0]
    batch, seq, d_model = x_prompt.shape
    dec_batch, dec_seq, _ = x_sample.shape
    assert batch == 1 and dec_seq == 1
    ret_dk = d_model // RET_HEADS
    ret_dv = 2 * d_model // RET_HEADS
    att_dh2 = d_model // ATT_HEADS

    yp = x_prompt.reshape(seq, d_model)
    ys = x_sample.reshape(dec_batch, d_model)
    ret_p, ret_s, kp_rows, vp_rows, ks_rows, vs_rows = [], [], [], [], [], []
    for i in range(depth):
        r = i // N_MIXERS
        last = i == depth - 1
        wgu0, wd0 = _prep_ffn_weights(ffn_w_gu[i, 0], ffn_w_d[i, 0])
        wgu1, wd1 = _prep_ffn_weights(ffn_w_gu[i, 1], ffn_w_d[i, 1])
        yp = _ffn(yp, norm_g[i, 0], wgu0, wd0, final_g, tm=ROW_TILE, final=False)
        ys = _ffn(ys, norm_g[i, 0], wgu0, wd0, final_g, tm=ROW_TILE, final=False)
        if i % N_MIXERS == 0:
            w_in = ret_w_in[r].astype(BF16)
            w_out = ret_w_out[r].astype(BF16)
            k_scale = ret_dk ** -0.5
            qkvg_p = _ret_proj(yp, norm_g[i, 1], w_in, tm=ROW_TILE, out_dtype=BF16, k_scale=k_scale)
            qkvg_s = _ret_proj(ys, norm_g[i, 1], w_in, tm=ROW_TILE, out_dtype=F32, k_scale=k_scale)
            op, sp = _ret_scan(qkvg_p, ret_gn_g[r], ret_gn_b[r], dk=ret_dk, dv=ret_dv)
            osm, ss = _ret_step(qkvg_s, state_ret, r, ret_gn_g[r], ret_gn_b[r], dk=ret_dk, dv=ret_dv)
            ret_p.append(sp.reshape(1, RET_HEADS, ret_dk, ret_dv))
            ret_s.append(ss)
        else:
            w_in = att_w_in[r].astype(BF16)
            w_out = att_w_out[r].astype(BF16)
            lam_init = 0.8 - 0.6 * math.exp(-0.3 * i)
            lams = (att_lam_q1[r], att_lam_k1[r], att_lam_q2[r], att_lam_k2[r])
            kp, vp, qb, kb, vb = _att_proj(yp, norm_g[i, 1], w_in, tm=ROW_TILE, head_major=True)
            qs, ksm, vsm = _att_proj(ys, norm_g[i, 1], w_in, tm=ROW_TILE, head_major=False)
            op = _flash(qb, kb, vb, lams, att_subln_g[r], tq=FLASH_TILE, lam_init=lam_init)
            hs = (dec_batch, ATT_HEADS, att_dh2)
            osm = _dec_attn(qs.reshape(hs), ksm.reshape(hs), vsm.reshape(hs), cache_k, cache_v, r, page_table,
                            lams, att_subln_g[r], lam_init=lam_init, n_pg=DECODE_PAGES_PER_STEP)
            osm = osm.reshape(dec_batch, d_model)
            kp_rows.append(kp.reshape(1, seq, ATT_HEADS, att_dh2))
            vp_rows.append(vp.reshape(1, seq, ATT_HEADS, att_dh2))
            ks_rows.append(ksm.reshape(dec_batch, 1, ATT_HEADS, att_dh2))
            vs_rows.append(vsm.reshape(dec_batch, 1, ATT_HEADS, att_dh2))
        yp = _out_proj(yp, op, w_out, tm=ROW_TILE)
        ys = _out_proj(ys, osm, w_out, tm=ROW_TILE)
        yp = _ffn(yp, norm_g[i, 2], wgu1, wd1, final_g, tm=ROW_TILE, final=last)
        ys = _ffn(ys, norm_g[i, 2], wgu1, wd1, final_g, tm=ROW_TILE, final=last)
    return (yp.reshape(1, seq, d_model), ys.reshape(dec_batch, 1, d_model),
            jnp.stack(ret_p), jnp.stack(kp_rows), jnp.stack(vp_rows),
            jnp.stack(ret_s), jnp.stack(ks_rows), jnp.stack(vs_rows))
```

```python
import functools
import math

import numpy as np
import jax
import jax.numpy as jnp
from jax import lax
from jax.experimental import pallas as pl
from jax.experimental.pallas import tpu as pltpu

F32 = jnp.float32
BF16 = jnp.bfloat16

NORM_EPS = 1e-6
GN_EPS = 1e-5
NEG_BIG = -1e30
LOG2E = 1.4426950408889634
N_MIXERS = 2

RET_HEADS = 8
RET_CHUNK = 128
ATT_HEADS = 8

V7X_VMEM_BYTES = 64 * 1024 * 1024
MXU_TILE = 256
LANES = 128


def _vmem_limit(nbytes):
    return int(min(V7X_VMEM_BYTES - 8 * 1024 * 1024, max(32 * 1024 * 1024, nbytes)))


def _rms(x, g):
    return x * lax.rsqrt(jnp.mean(x * x, axis=-1, keepdims=True) + NORM_EPS) * g


def _resident(shape):
    zeros = (0,) * len(shape)
    return pl.BlockSpec(shape, lambda *_: zeros, pipeline_mode=pl.Buffered(1))


def _ffn_kernel(x_ref, g_ref, wgu_ref, wd_ref, fg_ref, o_ref, xn_ref, acc_ref, *, n_chunks, ck, final):
    x = x_ref[...]
    xn_ref[...] = _rms(x, g_ref[...]).astype(BF16)
    acc_ref[...] = jnp.zeros_like(acc_ref)

    def chunk(c, carry):
        gu = jnp.dot(xn_ref[...], wgu_ref[c], preferred_element_type=F32)
        gate = gu[:, :ck]
        up = gu[:, ck:]
        h = (jax.nn.silu(gate) * up).astype(BF16)
        acc_ref[...] += jnp.dot(h, wd_ref[c], preferred_element_type=F32)
        return carry

    lax.fori_loop(0, n_chunks, chunk, 0, unroll=True)
    y = x + 0.5 * acc_ref[...]
    if final:
        y = _rms(y, fg_ref[...])
    o_ref[...] = y


def _ffn(x, g, wgu_r, wd_r, final_g, *, tm, final):
    m, d = x.shape
    n_chunks, _, ck2 = wgu_r.shape
    ck = ck2 // 2
    tm = min(tm, m)
    est = (wgu_r.size + wd_r.size) * 2 + tm * d * (4 * 4 + 2 + 4) + tm * ck2 * 4 * 3
    return pl.pallas_call(
        functools.partial(_ffn_kernel, n_chunks=n_chunks, ck=ck, final=final),
        out_shape=jax.ShapeDtypeStruct((m, d), F32),
        grid=(m // tm,),
        in_specs=[
            pl.BlockSpec((tm, d), lambda i: (i, 0)),
            _resident((1, d)),
            _resident(wgu_r.shape),
            _resident(wd_r.shape),
            _resident((1, d)),
        ],
        out_specs=pl.BlockSpec((tm, d), lambda i: (i, 0)),
        scratch_shapes=[pltpu.VMEM((tm, d), BF16), pltpu.VMEM((tm, d), F32)],
        compiler_params=pltpu.CompilerParams(
            dimension_semantics=("arbitrary",), vmem_limit_bytes=_vmem_limit(est + (8 << 20))),
        name="ffn",
    )(x, g.reshape(1, d), wgu_r, wd_r, final_g.reshape(1, d))


def _ret_proj_kernel(x_ref, g_ref, w_ref, o_ref, *, d, k_scale):
    xn = _rms(x_ref[...], g_ref[...]).astype(BF16)
    n = w_ref.shape[1]
    for c in range(n // d):
        r = jnp.dot(xn, w_ref[:, c * d:(c + 1) * d], preferred_element_type=F32)
        if c == 1:
            r = r * k_scale
        o_ref[:, c * d:(c + 1) * d] = r.astype(o_ref.dtype)


def _ret_proj(x, g, w, *, tm, out_dtype, k_scale):
    m, d = x.shape
    n = w.shape[1]
    tm = min(tm, m)
    est = w.size * 2 + tm * d * 4 * 2 + tm * n * 4 * 2 + tm * d * 4 * 3
    return pl.pallas_call(
        functools.partial(_ret_proj_kernel, d=d, k_scale=k_scale),
        out_shape=jax.ShapeDtypeStruct((m, n), out_dtype),
        grid=(m // tm,),
        in_specs=[pl.BlockSpec((tm, d), lambda i: (i, 0)), _resident((1, d)), _resident(w.shape)],
        out_specs=pl.BlockSpec((tm, n), lambda i: (i, 0)),
        compiler_params=pltpu.CompilerParams(
            dimension_semantics=("arbitrary",), vmem_limit_bytes=_vmem_limit(est + (8 << 20))),
        name="ret_proj",
    )(x, g.reshape(1, d), w)


def _out_proj_kernel(y_ref, o_ref, w_ref, out_ref):
    out_ref[...] = y_ref[...] + jnp.dot(o_ref[...].astype(BF16), w_ref[...], preferred_element_type=F32)


def _out_proj(y, o, w, *, tm):
    m, d = y.shape
    kdim = o.shape[1]
    tm = min(tm, m)
    est = w.size * 2 + tm * d * 4 * 4 + tm * kdim * 4 * 2 + tm * d * 4
    return pl.pallas_call(
        _out_proj_kernel,
        out_shape=jax.ShapeDtypeStruct((m, d), F32),
        grid=(m // tm,),
        in_specs=[pl.BlockSpec((tm, d), lambda i: (i, 0)),
                  pl.BlockSpec((tm, kdim), lambda i: (i, 0)),
                  _resident(w.shape)],
        out_specs=pl.BlockSpec((tm, d), lambda i: (i, 0)),
        compiler_params=pltpu.CompilerParams(
            dimension_semantics=("arbitrary",), vmem_limit_bytes=_vmem_limit(est + (8 << 20))),
        name="out_proj",
    )(y, o, w)


def _ret_consts(chunk):
    lg = np.log1p(-np.exp2(-5.0 - np.arange(RET_HEADS, dtype=np.float64)))
    pos = np.arange(chunk, dtype=np.float64)
    n = pos[:, None] - pos[None, :]
    decay = np.where(n[None] >= 0, np.exp(lg[:, None, None] * np.maximum(n, 0.0)[None]), 0.0)
    q_dec = np.exp(lg[:, None] * (pos[None, :] + 1.0))
    k_dec = np.exp(lg[:, None] * (chunk - 1.0 - pos)[None, :])
    s_dec = np.exp(lg * chunk)
    return decay, q_dec, k_dec, s_dec


def _group_norm_gate(o, gate, gn_g, gn_b):
    mu = jnp.mean(o, axis=-1, keepdims=True)
    var = jnp.mean(jnp.square(o - mu), axis=-1, keepdims=True)
    on = (o - mu) * lax.rsqrt(var + GN_EPS)
    on = on * gn_g + gn_b
    return on * jax.nn.silu(gate)


def _ret_scan_kernel(q_ref, k_ref, v_ref, gt_ref, dec_ref, qd_ref, kd_ref, gng_ref, gnb_ref,
                     o_ref, s_out_ref, s_ref, *, dk, dv, s_dec):
    t = pl.program_id(0)

    @pl.when(t == 0)
    def _():
        s_ref[...] = jnp.zeros_like(s_ref)

    for h in range(RET_HEADS):
        qh = q_ref[:, h * dk:(h + 1) * dk]
        kh = k_ref[:, h * dk:(h + 1) * dk]
        vh = v_ref[:, h * dv:(h + 1) * dv]
        sc = lax.dot_general(qh, kh, (((1,), (1,)), ((), ())), preferred_element_type=F32) * dec_ref[h]
        o = jnp.dot(sc.astype(BF16), vh, preferred_element_type=F32)
        s_old = s_ref[h]
        qd = (qh.astype(F32) * qd_ref[h]).astype(BF16)
        o = o + jnp.dot(qd, s_old.astype(BF16), preferred_element_type=F32)
        kd = (kh.astype(F32) * kd_ref[h]).astype(BF16)
        s_ref[h] = s_dec[h] * s_old + lax.dot_general(kd, vh, (((0,), (0,)), ((), ())),
                                                      preferred_element_type=F32)
        gate = gt_ref[:, h * dv:(h + 1) * dv].astype(F32)
        res = _group_norm_gate(o, gate, gng_ref[:, h * dv:(h + 1) * dv], gnb_ref[:, h * dv:(h + 1) * dv])
        o_ref[:, h * dv:(h + 1) * dv] = res.astype(o_ref.dtype)

    @pl.when(t == pl.num_programs(0) - 1)
    def _():
        s_out_ref[...] = s_ref[...]


def _ret_scan(qkvg, gn_g, gn_b, *, dk, dv):
    t_len = qkvg.shape[0]
    c = RET_CHUNK if t_len % RET_CHUNK == 0 else t_len
    hk, hv = RET_HEADS * dk, RET_HEADS * dv
    decay, q_dec, k_dec, s_dec = _ret_consts(c)
    dec = jnp.asarray(decay, F32)
    qd = jnp.asarray(np.broadcast_to(q_dec[:, :, None], (RET_HEADS, c, dk)), F32)
    kd = jnp.asarray(np.broadcast_to(k_dec[:, :, None], (RET_HEADS, c, dk)), F32)
    kq = hv // hk
    return pl.pallas_call(
        functools.partial(_ret_scan_kernel, dk=dk, dv=dv, s_dec=tuple(float(s) for s in s_dec)),
        out_shape=(jax.ShapeDtypeStruct((t_len, hv), BF16),
                   jax.ShapeDtypeStruct((RET_HEADS, dk, dv), F32)),
        grid=(t_len // c,),
        in_specs=[
            pl.BlockSpec((c, hk), lambda t: (t, 0)),
            pl.BlockSpec((c, hk), lambda t: (t, 1)),
            pl.BlockSpec((c, hv), lambda t: (t, 1)),
            pl.BlockSpec((c, hv), lambda t: (t, 2)),
            _resident(dec.shape), _resident(qd.shape), _resident(kd.shape),
            _resident((1, hv)), _resident((1, hv)),
        ],
        out_specs=(pl.BlockSpec((c, hv), lambda t: (t, 0)),
                   pl.BlockSpec((RET_HEADS, dk, dv), lambda t: (0, 0, 0))),
        scratch_shapes=[pltpu.VMEM((RET_HEADS, dk, dv), F32)],
        compiler_params=pltpu.CompilerParams(dimension_semantics=("arbitrary",)),
        name="ret_scan",
    )(qkvg, qkvg, qkvg, qkvg, dec, qd, kd, gn_g.reshape(1, hv), gn_b.reshape(1, hv))


def _ret_step_kernel(x_ref, s_ref, gng_ref, gnb_ref, o_ref, s_out_ref, *, dk, dv, gamma):
    hk = RET_HEADS * dk
    hv = RET_HEADS * dv
    for h in range(RET_HEADS):
        q = x_ref[:, h * dk:(h + 1) * dk]
        k = x_ref[:, hk + h * dk:hk + (h + 1) * dk]
        v = x_ref[:, 2 * hk + h * dv:2 * hk + (h + 1) * dv]
        gate = x_ref[:, 2 * hk + hv + h * dv:2 * hk + hv + (h + 1) * dv]
        q_col = jnp.transpose(jnp.broadcast_to(q, (dk, dk)))[:, :1]
        k_col = jnp.transpose(jnp.broadcast_to(k, (dk, dk)))[:, :1]
        s_old = s_ref[h]
        qk = jnp.sum(q * k, axis=1, keepdims=True)
        o = qk * v + jnp.sum((q_col * gamma[h]) * s_old, axis=0, keepdims=True)
        s_out_ref[h] = gamma[h] * s_old + k_col * v
        res = _group_norm_gate(o, gate, gng_ref[:, h * dv:(h + 1) * dv], gnb_ref[:, h * dv:(h + 1) * dv])
        o_ref[:, h * dv:(h + 1) * dv] = res


def _ret_step(qkvg, state, layer, gn_g, gn_b, *, dk, dv):
    b, n = qkvg.shape
    hv = RET_HEADS * dv
    gamma = tuple(float(1.0 - 2.0 ** (-5.0 - h)) for h in range(RET_HEADS))
    o, s_new = pl.pallas_call(
        functools.partial(_ret_step_kernel, dk=dk, dv=dv, gamma=gamma),
        out_shape=(jax.ShapeDtypeStruct((b, 1, hv), F32),
                   jax.ShapeDtypeStruct(state.shape[1:], F32)),
        grid=(b,),
        in_specs=[
            pl.BlockSpec((None, 1, n), lambda i: (i, 0, 0)),
            pl.BlockSpec((None, None, RET_HEADS, dk, dv), lambda i: (layer, i, 0, 0, 0)),
            _resident((1, hv)), _resident((1, hv)),
        ],
        out_specs=(pl.BlockSpec((None, 1, hv), lambda i: (i, 0, 0)),
                   pl.BlockSpec((None, RET_HEADS, dk, dv), lambda i: (i, 0, 0, 0))),
        compiler_params=pltpu.CompilerParams(dimension_semantics=("arbitrary",)),
        name="ret_step",
    )(qkvg.reshape(b, 1, n), state, gn_g.reshape(1, hv), gn_b.reshape(1, hv))
    return o.reshape(b, hv), s_new


def _att_proj_kernel(x_ref, g_ref, w_ref, *out_refs, d, dh2, q_scale, head_major):
    xn = _rms(x_ref[...], g_ref[...]).astype(BF16)
    q = jnp.dot(xn, w_ref[:, 0:d], preferred_element_type=F32) * q_scale
    k = jnp.dot(xn, w_ref[:, d:2 * d], preferred_element_type=F32)
    v = jnp.dot(xn, w_ref[:, 2 * d:3 * d], preferred_element_type=F32)
    if head_major:
        k_ref, v_ref, qt_ref, kb_ref, vt_ref = out_refs
        k_ref[...] = k
        v_ref[...] = v
        for h in range(ATT_HEADS):
            qt_ref[h, 0] = jnp.transpose(q[:, h * dh2:(h + 1) * dh2]).astype(BF16)
            kb_ref[h] = k[:, h * dh2:(h + 1) * dh2].astype(BF16)
            vt_ref[h, 0] = jnp.transpose(v[:, h * dh2:(h + 1) * dh2]).astype(BF16)
    else:
        q_ref, k_ref, v_ref = out_refs
        q_ref[...] = q
        k_ref[...] = k
        v_ref[...] = v


def _att_proj(x, g, w, *, tm, head_major):
    m, d = x.shape
    dh2 = d // ATT_HEADS
    tm = min(tm, m)
    row = pl.BlockSpec((tm, d), lambda i: (i, 0))
    if head_major:
        hm = pl.BlockSpec((ATT_HEADS, tm, dh2), lambda i: (0, i, 0))
        hm_t = pl.BlockSpec((ATT_HEADS, 1, dh2, tm), lambda i: (0, i, 0, 0))
        t_shape = jax.ShapeDtypeStruct((ATT_HEADS, m // tm, dh2, tm), BF16)
        out_shape = (jax.ShapeDtypeStruct((m, d), F32),) * 2 + (
            t_shape, jax.ShapeDtypeStruct((ATT_HEADS, m, dh2), BF16), t_shape)
        out_specs = (row, row, hm_t, hm, hm_t)
    else:
        out_shape = (jax.ShapeDtypeStruct((m, d), F32),) * 3
        out_specs = (row, row, row)
    est = w.size * 2 + tm * d * 4 * 12
    return pl.pallas_call(
        functools.partial(_att_proj_kernel, d=d, dh2=dh2, q_scale=(dh2 // 2) ** -0.5 * LOG2E,
                          head_major=head_major),
        out_shape=out_shape,
        grid=(m // tm,),
        in_specs=[row, _resident((1, d)), _resident(w.shape)],
        out_specs=out_specs,
        compiler_params=pltpu.CompilerParams(
            dimension_semantics=("arbitrary",), vmem_limit_bytes=_vmem_limit(est + (8 << 20))),
        name="att_proj",
    )(x, g.reshape(1, d), w)


def _alibi_slopes():
    return np.exp2(-8.0 * (np.arange(ATT_HEADS, dtype=np.float64) + 1.0) / ATT_HEADS)


def _lambda(lq1_ref, lk1_ref, lq2_ref, lk2_ref, lam_init):
    a = jnp.sum(lq1_ref[...] * lk1_ref[...], axis=1, keepdims=True)
    b = jnp.sum(lq2_ref[...] * lk2_ref[...], axis=1, keepdims=True)
    return jnp.exp(a) - jnp.exp(b) + lam_init


def _sub_norm(o, sub_g, lam_init):
    return _rms(o, sub_g) * (1.0 - lam_init)


def _flash_kernel(lq1_ref, lk1_ref, lq2_ref, lk2_ref, subg_ref, slope_ref, qt_ref, k_ref, vt_ref, o_ref,
                  qz_ref, bias_ref, sa_ref, sb_ref, m_ref, acc_ref, *, tq, tk, dh, lam_init):
    i = pl.program_id(1)
    dv = 2 * dh
    ratio = tq // tk
    n_kt = k_ref.shape[0] // tk
    slope2 = slope_ref[:, :1] * LOG2E

    @pl.when(i == 0)
    def _():
        key = lax.broadcasted_iota(jnp.int32, (tk, tq), 0)
        qry = lax.broadcasted_iota(jnp.int32, (tk, tq), 1)
        alibi = key.astype(F32) * slope2
        bias_ref[0] = alibi
        for d in range(ratio):
            bias_ref[d + 1] = jnp.where(key + d * tk <= qry, alibi, NEG_BIG)

    qt = jnp.concatenate([qt_ref[b] for b in range(qt_ref.shape[0])], axis=1)
    feat = lax.broadcasted_iota(jnp.int32, qt.shape, 0)
    zero = jnp.zeros_like(qt)
    qz_ref[:, :tq] = jnp.where(feat < dh, qt, zero)
    qz_ref[:, tq:] = jnp.where(feat >= dh, qt, zero)
    m_ref[...] = jnp.full_like(m_ref, NEG_BIG)
    acc_ref[...] = jnp.zeros_like(acc_ref)
    ones = jnp.ones((dv, tk), BF16)

    def tile_index(j):
        return jnp.minimum(j, n_kt - 1)

    def scores(j, s_ref):
        kt = k_ref[pl.ds(pl.multiple_of(tile_index(j) * tk, tk), tk), :]
        bias = bias_ref[jnp.clip(j - ratio * i + 1, 0, ratio)]
        s_ref[...] = (jnp.dot(kt, qz_ref[...], preferred_element_type=F32)
                      + jnp.concatenate([bias, bias], axis=1))

    def consume(j, s_ref):
        va = jnp.concatenate([vt_ref[tile_index(j)], ones], axis=0)
        blk = jnp.full((1, 1), j * tk - i * tq, jnp.int32).astype(F32) * slope2
        s = s_ref[...]
        m_old = m_ref[...]
        m_new = jnp.maximum(m_old, jnp.max(s, axis=0, keepdims=True) + blk)
        p = jnp.exp2(s - (m_new - blk))
        alpha = jnp.exp2(m_old - m_new)
        acc_ref[...] = alpha * acc_ref[...] + jnp.dot(va, p.astype(BF16), preferred_element_type=F32)
        m_ref[...] = m_new

    scores(0, sa_ref)

    def pair(t, carry):
        a = 2 * t
        scores(a + 1, sb_ref)
        consume(a, sa_ref)
        scores(a + 2, sa_ref)
        consume(a + 1, sb_ref)
        return carry

    lax.fori_loop(0, (ratio * (i + 1)) // 2, pair, 0)

    lam = _lambda(lq1_ref, lk1_ref, lq2_ref, lk2_ref, lam_init)
    a1 = acc_ref[:, :tq]
    a2 = acc_ref[:, tq:]
    o = a1[:dv] / a1[dv:dv + 1] - lam * (a2[:dv] / a2[dv:dv + 1])
    on = o * lax.rsqrt(jnp.mean(o * o, axis=0, keepdims=True) + NORM_EPS) * subg_ref[...] * (1.0 - lam_init)
    o_ref[...] = jnp.transpose(on).astype(o_ref.dtype)


def _flash(qt, kb, vt, lams, sub_g, *, tq, lam_init):
    h, n_kt, dh2, tk = vt.shape
    t_len = kb.shape[1]
    dh = dh2 // 2
    tq = max(min(tq, t_len), 2 * tk)
    ratio = tq // tk
    assert tq % tk == 0 and ratio % 2 == 0 and t_len % tq == 0
    slopes = jnp.asarray(np.broadcast_to(_alibi_slopes()[:, None, None], (h, 1, LANES)), F32)
    lam_specs = [_resident((1, dh))] * 4
    est = (2 * 2 * t_len * dh2 * 2 + (ratio + 1) * tq * tk * 4 + 2 * tq * tk * 4 * 5
           + 2 * tq * 2 * dh2 * 4 * 3 + 4 * tq * dh2 * 2 * 2)
    return pl.pallas_call(
        functools.partial(_flash_kernel, tq=tq, tk=tk, dh=dh, lam_init=lam_init),
        out_shape=jax.ShapeDtypeStruct((t_len, h * dh2), BF16),
        grid=(h, t_len // tq),
        in_specs=lam_specs + [
            _resident((dh2, 1)),
            pl.BlockSpec((None, 1, LANES), lambda hh, i: (hh, 0, 0)),
            pl.BlockSpec((None, ratio, dh2, tk), lambda hh, i: (hh, i, 0, 0)),
            pl.BlockSpec((None, t_len, dh2), lambda hh, i: (hh, 0, 0)),
            pl.BlockSpec((None, n_kt, dh2, tk), lambda hh, i: (hh, 0, 0, 0)),
        ],
        out_specs=pl.BlockSpec((tq, dh2), lambda hh, i: (i, hh)),
        scratch_shapes=[pltpu.VMEM((dh2, 2 * tq), BF16), pltpu.VMEM((ratio + 1, tk, tq), F32),
                        pltpu.VMEM((tk, 2 * tq), F32), pltpu.VMEM((tk, 2 * tq), F32),
                        pltpu.VMEM((1, 2 * tq), F32), pltpu.VMEM((2 * dh2, 2 * tq), F32)],
        compiler_params=pltpu.CompilerParams(
            dimension_semantics=("arbitrary", "arbitrary"), vmem_limit_bytes=_vmem_limit(est + (8 << 20))),
        name="flash_diff_attn",
    )(*[a.reshape(1, dh) for a in lams], sub_g.reshape(dh2, 1), slopes, qt, kb, vt)


def _dec_attn_kernel(pt_ref, lq1_ref, lk1_ref, lq2_ref, lk2_ref, subg_ref, slope_ref, feat_ref, hs_ref,
                     q_ref, kn_ref, vn_ref, *rest, n_pg, page, past_len, dh, lam_init):
    del pt_ref
    k_refs = rest[:n_pg]
    v_refs = rest[n_pg:2 * n_pg]
    o_ref, m_ref, l_ref, acc_ref = rest[2 * n_pg:]
    g = pl.program_id(1)
    dh2 = 2 * dh
    nh = q_ref.shape[0]

    @pl.when(g == 0)
    def _():
        m_ref[...] = jnp.full_like(m_ref, NEG_BIG)
        l_ref[...] = jnp.zeros_like(l_ref)
        acc_ref[...] = jnp.zeros_like(acc_ref)

    q = q_ref[...]
    slope2 = slope_ref[...] * LOG2E
    feat = feat_ref[...]
    hs = hs_ref[...]

    def update(s, c, off, v):
        m_old = m_ref[c]
        m_new = jnp.maximum(m_old, jnp.max(s, axis=0) + off)
        p = jnp.exp2(s - (m_new - off)[None])
        alpha = jnp.exp2(m_old - m_new)
        l_ref[c] = alpha * l_ref[c] + jnp.sum(p, axis=0)
        acc_ref[c] = alpha * acc_ref[c] + jnp.sum(p * v, axis=0)
        m_ref[c] = m_new

    def over_positions(x, op):
        parts = op(x.reshape(POSITION_CHAINS, page // POSITION_CHAINS, nh, x.shape[-1]), axis=1)
        return op(parts, axis=0)

    sbs, offs = [], []
    for pi in range(n_pg):
        kp = k_refs[pi][...]
        prod = (kp * q[None]).astype(BF16).reshape(page * nh, dh2)
        sbs.append(jnp.dot(jnp.concatenate([prod, feat], axis=1), hs, preferred_element_type=F32))
        first_pos = (g * n_pg + pi) * page - past_len
        offs.append(jnp.full((1, 1), first_pos, jnp.int32).astype(F32) * slope2)
    for c in range(2):
        ss = [sb[:, c * LANES:(c + 1) * LANES].reshape(page, nh, LANES) for sb in sbs]
        m_old = m_ref[c]
        m_new = m_old
        for s, off in zip(ss, offs):
            m_new = jnp.maximum(m_new, over_positions(s, jnp.max) + off)
        l_add = jnp.zeros_like(m_old)
        a_add = jnp.zeros((nh, dh2), F32)
        for s, off, v_ref in zip(ss, offs, v_refs):
            p = jnp.exp2(s - (m_new - off)[None])
            l_add = l_add + over_positions(p, jnp.sum)
            a_add = a_add + over_positions(p * v_ref[...], jnp.sum)
        alpha = jnp.exp2(m_old - m_new)
        l_ref[c] = alpha * l_ref[c] + l_add
        acc_ref[c] = alpha * acc_ref[c] + a_add
        m_ref[c] = m_new

    @pl.when(g == pl.num_programs(1) - 1)
    def _():
        kn = kn_ref[...]
        vn = vn_ref[...]
        ss = jnp.dot((kn * q).astype(BF16), hs[:dh2], preferred_element_type=F32)
        zero_off = jnp.zeros_like(slope2)
        for c in range(2):
            update(ss[:, c * LANES:(c + 1) * LANES][None], c, zero_off, vn[None])
        lam = _lambda(lq1_ref, lk1_ref, lq2_ref, lk2_ref, lam_init)
        o = acc_ref[0] / l_ref[0] - lam * (acc_ref[1] / l_ref[1])
        o_ref[...] = _sub_norm(o, subg_ref[...], lam_init)


def _dec_attn(q, kn, vn, cache_k, cache_v, layer, page_table, lams, sub_g, *, lam_init, n_pg):
    b, nh, dh2 = q.shape
    dh = dh2 // 2
    n_pages = page_table.shape[1]
    page = cache_k.shape[2]
    while n_pages % n_pg:
        n_pg //= 2
    slopes = jnp.asarray(np.broadcast_to(_alibi_slopes()[:, None], (nh, LANES)), F32)
    rest = (np.arange(page, dtype=np.float64)[:, None] * (_alibi_slopes() * LOG2E)[None, :]).astype(np.float32)
    feat_np = np.zeros((page * nh, LANES), np.float32)
    for col in range(3):
        part = rest.astype(BF16).astype(np.float32)
        feat_np[:, col] = part.reshape(-1)
        rest = rest - part
    feat = jnp.asarray(feat_np, BF16)
    hs_np = np.zeros((dh2 + LANES, 2 * LANES), np.float32)
    hs_np[:dh, :LANES] = 1.0
    hs_np[dh:dh2, LANES:] = 1.0
    hs_np[dh2:dh2 + 3, :] = 1.0
    hs = jnp.asarray(hs_np, BF16)

    def page_spec(pi):
        return pl.BlockSpec((None, None, page, nh, dh2),
                            lambda bb, g, pt: (layer, pt[bb, g * n_pg + pi], 0, 0, 0))

    tok = pl.BlockSpec((None, nh, dh2), lambda bb, g, pt: (bb, 0, 0))

    def const(shape):
        zeros = (0,) * len(shape)
        return pl.BlockSpec(shape, lambda bb, g, pt: zeros)

    grid_spec = pltpu.PrefetchScalarGridSpec(
        num_scalar_prefetch=1,
        grid=(b, n_pages // n_pg),
        in_specs=[const((1, dh))] * 4 + [const((1, dh2)), const((nh, LANES)), const(feat.shape), const(hs.shape),
                                         tok, tok, tok]
                 + [page_spec(pi) for pi in range(n_pg)] * 2,
        out_specs=tok,
        scratch_shapes=[pltpu.VMEM((2, nh, LANES), F32), pltpu.VMEM((2, nh, LANES), F32),
                        pltpu.VMEM((2, nh, dh2), F32)],
    )
    est = 2 * 2 * n_pg * page * nh * dh2 * 4 + page * nh * LANES * 4 * 12
    return pl.pallas_call(
        functools.partial(_dec_attn_kernel, n_pg=n_pg, page=page, past_len=n_pages * page, dh=dh,
                          lam_init=lam_init),
        out_shape=jax.ShapeDtypeStruct((b, nh, dh2), F32),
        grid_spec=grid_spec,
        compiler_params=pltpu.CompilerParams(
            dimension_semantics=("arbitrary", "arbitrary"), vmem_limit_bytes=_vmem_limit(est + (8 << 20))),
        name="decode_diff_attn",
    )(page_table, *[a.reshape(1, dh) for a in lams], sub_g.reshape(1, dh2), slopes, feat, hs, q, kn, vn,
      *([cache_k] * n_pg), *([cache_v] * n_pg))


FFN_CHUNK = MXU_TILE
ROW_TILE = 512
FLASH_Q_TILE = 1024
FLASH_K_TILE = 512
DECODE_PAGES_PER_STEP = 4
POSITION_CHAINS = 8


def _prep_ffn_weights(w_gu, w_d):
    d, two_ff = w_gu.shape
    d_ff = two_ff // 2
    n_chunks = d_ff // FFN_CHUNK
    gate = w_gu[:, :d_ff].reshape(d, n_chunks, 1, FFN_CHUNK)
    up = w_gu[:, d_ff:].reshape(d, n_chunks, 1, FFN_CHUNK)
    wgu_r = jnp.concatenate([gate, up], axis=2).transpose(1, 0, 2, 3).reshape(n_chunks, d, 2 * FFN_CHUNK)
    return wgu_r.astype(BF16), w_d.reshape(n_chunks, FFN_CHUNK, d).astype(BF16)


def kernel(x_prompt, x_sample, state_ret, cache_k, cache_v, page_table, norm_g, ffn_w_gu, ffn_w_d, final_g,
           ret_w_in, ret_w_out, ret_gn_g, ret_gn_b, att_w_in, att_w_out, att_lam_q1, att_lam_k1, att_lam_q2,
           att_lam_k2, att_subln_g):
    depth = norm_g.shape[0]
    batch, seq, d_model = x_prompt.shape
    dec_batch, dec_seq, _ = x_sample.shape
    assert batch == 1 and dec_seq == 1
    ret_dk = d_model // RET_HEADS
    ret_dv = 2 * d_model // RET_HEADS
    att_dh2 = d_model // ATT_HEADS

    yp = x_prompt.reshape(seq, d_model)
    ys = x_sample.reshape(dec_batch, d_model)
    ret_p, ret_s, kp_rows, vp_rows, ks_rows, vs_rows = [], [], [], [], [], []
    for i in range(depth):
        r = i // N_MIXERS
        last = i == depth - 1
        wgu0, wd0 = _prep_ffn_weights(ffn_w_gu[i, 0], ffn_w_d[i, 0])
        wgu1, wd1 = _prep_ffn_weights(ffn_w_gu[i, 1], ffn_w_d[i, 1])
        yp = _ffn(yp, norm_g[i, 0], wgu0, wd0, final_g, tm=ROW_TILE, final=False)
        ys = _ffn(ys, norm_g[i, 0], wgu0, wd0, final_g, tm=ROW_TILE, final=False)
        if i % N_MIXERS == 0:
            w_in = ret_w_in[r].astype(BF16)
            w_out = ret_w_out[r].astype(BF16)
            k_scale = ret_dk ** -0.5
            qkvg_p = _ret_proj(yp, norm_g[i, 1], w_in, tm=ROW_TILE, out_dtype=BF16, k_scale=k_scale)
            qkvg_s = _ret_proj(ys, norm_g[i, 1], w_in, tm=ROW_TILE, out_dtype=F32, k_scale=k_scale)
            op, sp = _ret_scan(qkvg_p, ret_gn_g[r], ret_gn_b[r], dk=ret_dk, dv=ret_dv)
            osm, ss = _ret_step(qkvg_s, state_ret, r, ret_gn_g[r], ret_gn_b[r], dk=ret_dk, dv=ret_dv)
            ret_p.append(sp.reshape(1, RET_HEADS, ret_dk, ret_dv))
            ret_s.append(ss)
        else:
            w_in = att_w_in[r].astype(BF16)
            w_out = att_w_out[r].astype(BF16)
            lam_init = 0.8 - 0.6 * math.exp(-0.3 * i)
            lams = (att_lam_q1[r], att_lam_k1[r], att_lam_q2[r], att_lam_k2[r])
            kp, vp, qt, kb, vt = _att_proj(yp, norm_g[i, 1], w_in, tm=FLASH_K_TILE, head_major=True)
            qs, ksm, vsm = _att_proj(ys, norm_g[i, 1], w_in, tm=ROW_TILE, head_major=False)
            op = _flash(qt, kb, vt, lams, att_subln_g[r], tq=FLASH_Q_TILE, lam_init=lam_init)
            hs = (dec_batch, ATT_HEADS, att_dh2)
            osm = _dec_attn(qs.reshape(hs), ksm.reshape(hs), vsm.reshape(hs), cache_k, cache_v, r, page_table,
                            lams, att_subln_g[r], lam_init=lam_init, n_pg=DECODE_PAGES_PER_STEP)
            osm = osm.reshape(dec_batch, d_model)
            kp_rows.append(kp.reshape(1, seq, ATT_HEADS, att_dh2))
            vp_rows.append(vp.reshape(1, seq, ATT_HEADS, att_dh2))
            ks_rows.append(ksm.reshape(dec_batch, 1, ATT_HEADS, att_dh2))
            vs_rows.append(vsm.reshape(dec_batch, 1, ATT_HEADS, att_dh2))
        yp = _out_proj(yp, op, w_out, tm=ROW_TILE)
        ys = _out_proj(ys, osm, w_out, tm=ROW_TILE)
        yp = _ffn(yp, norm_g[i, 2], wgu1, wd1, final_g, tm=ROW_TILE, final=last)
        ys = _ffn(ys, norm_g[i, 2], wgu1, wd1, final_g, tm=ROW_TILE, final=last)
    return (yp.reshape(1, seq, d_model), ys.reshape(dec_batch, 1, d_model),
            jnp.stack(ret_p), jnp.stack(kp_rows), jnp.stack(vp_rows),
            jnp.stack(ret_s), jnp.stack(ks_rows), jnp.stack(vs_rows))
```

```python
import functools
import math

import numpy as np
import jax
import jax.numpy as jnp
from jax import lax
from jax.experimental import pallas as pl
from jax.experimental.pallas import tpu as pltpu

F32 = jnp.float32
BF16 = jnp.bfloat16

NORM_EPS = 1e-6
GN_EPS = 1e-5
NEG_BIG = -1e30
LOG2E = 1.4426950408889634
N_MIXERS = 2

RET_HEADS = 8
RET_CHUNK = 128
RET_CHUNKS_PER_STEP = 4
ATT_HEADS = 8

V7X_VMEM_BYTES = 64 * 1024 * 1024
MXU_TILE = 256
LANES = 128
BF16_SUBLANES = 16
N_BIAS_FEATURES = 3


def _vmem_limit(nbytes):
    return int(min(V7X_VMEM_BYTES - 8 * 1024 * 1024, max(32 * 1024 * 1024, nbytes)))


def _rms(x, g):
    return x * lax.rsqrt(jnp.mean(x * x, axis=-1, keepdims=True) + NORM_EPS) * g


def _resident(shape):
    zeros = (0,) * len(shape)
    return pl.BlockSpec(shape, lambda *_: zeros, pipeline_mode=pl.Buffered(1))


def _ffn_kernel(x_ref, g_ref, wgu_ref, wd_ref, fg_ref, o_ref, xn_ref, acc_ref, *, d_ff, ck, final):
    x = x_ref[...]
    xn_ref[...] = _rms(x, g_ref[...]).astype(BF16)
    acc_ref[...] = jnp.zeros_like(acc_ref)
    for c0 in range(0, d_ff, ck):
        xn = xn_ref[...]
        gate = jnp.dot(xn, wgu_ref[:, c0:c0 + ck], preferred_element_type=F32)
        up = jnp.dot(xn, wgu_ref[:, d_ff + c0:d_ff + c0 + ck], preferred_element_type=F32)
        h = (jax.nn.silu(gate) * up).astype(BF16)
        acc_ref[...] += jnp.dot(h, wd_ref[c0:c0 + ck, :], preferred_element_type=F32)
    y = x + 0.5 * acc_ref[...]
    if final:
        y = _rms(y, fg_ref[...])
    o_ref[...] = y


def _ffn(x, g, wgu_r, wd_r, final_g, *, tm, final):
    m, d = x.shape
    d_ff = wd_r.shape[0]
    ck = FFN_CHUNK
    tm = min(tm, m)
    est = (wgu_r.size + wd_r.size) * 2 + tm * d * (4 * 4 + 2 + 4) + tm * 2 * ck * 4 * 3
    return pl.pallas_call(
        functools.partial(_ffn_kernel, d_ff=d_ff, ck=ck, final=final),
        out_shape=jax.ShapeDtypeStruct((m, d), F32),
        grid=(m // tm,),
        in_specs=[
            pl.BlockSpec((tm, d), lambda i: (i, 0)),
            _resident((1, d)),
            _resident(wgu_r.shape),
            _resident(wd_r.shape),
            _resident((1, d)),
        ],
        out_specs=pl.BlockSpec((tm, d), lambda i: (i, 0)),
        scratch_shapes=[pltpu.VMEM((tm, d), BF16), pltpu.VMEM((tm, d), F32)],
        compiler_params=pltpu.CompilerParams(
            dimension_semantics=("arbitrary",), vmem_limit_bytes=_vmem_limit(est + (8 << 20))),
        name="ffn",
    )(x, g.reshape(1, d), wgu_r, wd_r, final_g.reshape(1, d))


def _ret_proj_kernel(x_ref, g_ref, w_ref, o_ref, *, d, k_scale):
    xn = _rms(x_ref[...], g_ref[...]).astype(BF16)
    n = w_ref.shape[1]
    for c in range(n // d):
        r = jnp.dot(xn, w_ref[:, c * d:(c + 1) * d], preferred_element_type=F32)
        if c == 1:
            r = r * k_scale
        o_ref[:, c * d:(c + 1) * d] = r.astype(o_ref.dtype)


def _ret_proj(x, g, w, *, tm, out_dtype, k_scale):
    m, d = x.shape
    n = w.shape[1]
    tm = min(tm, m)
    est = w.size * 2 + tm * d * 4 * 2 + tm * n * 4 * 2 + tm * d * 4 * 3
    return pl.pallas_call(
        functools.partial(_ret_proj_kernel, d=d, k_scale=k_scale),
        out_shape=jax.ShapeDtypeStruct((m, n), out_dtype),
        grid=(m // tm,),
        in_specs=[pl.BlockSpec((tm, d), lambda i: (i, 0)), _resident((1, d)), _resident(w.shape)],
        out_specs=pl.BlockSpec((tm, n), lambda i: (i, 0)),
        compiler_params=pltpu.CompilerParams(
            dimension_semantics=("arbitrary",), vmem_limit_bytes=_vmem_limit(est + (8 << 20))),
        name="ret_proj",
    )(x, g.reshape(1, d), w)


def _out_proj_kernel(y_ref, o_ref, w_ref, out_ref):
    out_ref[...] = y_ref[...] + jnp.dot(o_ref[...].astype(BF16), w_ref[...], preferred_element_type=F32)


def _out_proj(y, o, w, *, tm):
    m, d = y.shape
    kdim = o.shape[1]
    tm = min(tm, m)
    est = w.size * 2 + tm * d * 4 * 4 + tm * kdim * 4 * 2 + tm * d * 4
    return pl.pallas_call(
        _out_proj_kernel,
        out_shape=jax.ShapeDtypeStruct((m, d), F32),
        grid=(m // tm,),
        in_specs=[pl.BlockSpec((tm, d), lambda i: (i, 0)),
                  pl.BlockSpec((tm, kdim), lambda i: (i, 0)),
                  _resident(w.shape)],
        out_specs=pl.BlockSpec((tm, d), lambda i: (i, 0)),
        compiler_params=pltpu.CompilerParams(
            dimension_semantics=("arbitrary",), vmem_limit_bytes=_vmem_limit(est + (8 << 20))),
        name="out_proj",
    )(y, o, w)


def _ret_consts(chunk):
    lg = np.log1p(-np.exp2(-5.0 - np.arange(RET_HEADS, dtype=np.float64)))
    pos = np.arange(chunk, dtype=np.float64)
    n = pos[:, None] - pos[None, :]
    decay = np.where(n[None] >= 0, np.exp(lg[:, None, None] * np.maximum(n, 0.0)[None]), 0.0)
    q_dec = np.exp(lg[:, None] * (pos[None, :] + 1.0))
    k_dec = np.exp(lg[:, None] * (chunk - 1.0 - pos)[None, :])
    s_dec = np.exp(lg * chunk)
    return decay, q_dec, k_dec, s_dec


def _group_norm_gate(o, gate, gn_g, gn_b):
    mu = jnp.mean(o, axis=-1, keepdims=True)
    var = jnp.mean(jnp.square(o - mu), axis=-1, keepdims=True)
    on = (o - mu) * lax.rsqrt(var + GN_EPS)
    on = on * gn_g + gn_b
    return on * jax.nn.silu(gate)


def _ret_scan_kernel(q_ref, k_ref, v_ref, gt_ref, dec_ref, qd_ref, kd_ref, gng_ref, gnb_ref,
                     o_ref, s_out_ref, s_ref, *, dk, dv, s_dec):
    t = pl.program_id(0)

    @pl.when(t == 0)
    def _():
        s_ref[...] = jnp.zeros_like(s_ref)

    c = dec_ref.shape[1]
    for r0 in range(0, q_ref.shape[0], c):
        rows = slice(r0, r0 + c)
        for h in range(RET_HEADS):
            qh = q_ref[rows, h * dk:(h + 1) * dk]
            kh = k_ref[rows, h * dk:(h + 1) * dk]
            vh = v_ref[rows, h * dv:(h + 1) * dv]
            sc = lax.dot_general(qh, kh, (((1,), (1,)), ((), ())), preferred_element_type=F32) * dec_ref[h]
            o = jnp.dot(sc.astype(BF16), vh, preferred_element_type=F32)
            s_old = s_ref[h]
            qd = (qh.astype(F32) * qd_ref[h]).astype(BF16)
            o = o + jnp.dot(qd, s_old.astype(BF16), preferred_element_type=F32)
            kd = (kh.astype(F32) * kd_ref[h]).astype(BF16)
            s_ref[h] = s_dec[h] * s_old + lax.dot_general(kd, vh, (((0,), (0,)), ((), ())),
                                                          preferred_element_type=F32)
            gate = gt_ref[rows, h * dv:(h + 1) * dv].astype(F32)
            res = _group_norm_gate(o, gate, gng_ref[:, h * dv:(h + 1) * dv], gnb_ref[:, h * dv:(h + 1) * dv])
            o_ref[rows, h * dv:(h + 1) * dv] = res.astype(o_ref.dtype)

    @pl.when(t == pl.num_programs(0) - 1)
    def _():
        s_out_ref[...] = s_ref[...]


def _ret_scan(qkvg, gn_g, gn_b, *, dk, dv):
    t_len = qkvg.shape[0]
    c = RET_CHUNK if t_len % RET_CHUNK == 0 else t_len
    hk, hv = RET_HEADS * dk, RET_HEADS * dv
    decay, q_dec, k_dec, s_dec = _ret_consts(c)
    dec = jnp.asarray(decay, F32)
    qd = jnp.asarray(np.broadcast_to(q_dec[:, :, None], (RET_HEADS, c, dk)), F32)
    kd = jnp.asarray(np.broadcast_to(k_dec[:, :, None], (RET_HEADS, c, dk)), F32)
    rows = c * RET_CHUNKS_PER_STEP if t_len % (c * RET_CHUNKS_PER_STEP) == 0 else c
    return pl.pallas_call(
        functools.partial(_ret_scan_kernel, dk=dk, dv=dv, s_dec=tuple(float(s) for s in s_dec)),
        out_shape=(jax.ShapeDtypeStruct((t_len, hv), BF16),
                   jax.ShapeDtypeStruct((RET_HEADS, dk, dv), F32)),
        grid=(t_len // rows,),
        in_specs=[
            pl.BlockSpec((rows, hk), lambda t: (t, 0)),
            pl.BlockSpec((rows, hk), lambda t: (t, 1)),
            pl.BlockSpec((rows, hv), lambda t: (t, 1)),
            pl.BlockSpec((rows, hv), lambda t: (t, 2)),
            _resident(dec.shape), _resident(qd.shape), _resident(kd.shape),
            _resident((1, hv)), _resident((1, hv)),
        ],
        out_specs=(pl.BlockSpec((rows, hv), lambda t: (t, 0)),
                   pl.BlockSpec((RET_HEADS, dk, dv), lambda t: (0, 0, 0))),
        scratch_shapes=[pltpu.VMEM((RET_HEADS, dk, dv), F32)],
        compiler_params=pltpu.CompilerParams(dimension_semantics=("arbitrary",)),
        name="ret_scan",
    )(qkvg, qkvg, qkvg, qkvg, dec, qd, kd, gn_g.reshape(1, hv), gn_b.reshape(1, hv))


def _ret_step_kernel(x_ref, s_ref, gng_ref, gnb_ref, o_ref, s_out_ref, *, dk, dv, gamma):
    hk = RET_HEADS * dk
    hv = RET_HEADS * dv
    for h in range(RET_HEADS):
        q = x_ref[:, h * dk:(h + 1) * dk]
        k = x_ref[:, hk + h * dk:hk + (h + 1) * dk]
        v = x_ref[:, 2 * hk + h * dv:2 * hk + (h + 1) * dv]
        gate = x_ref[:, 2 * hk + hv + h * dv:2 * hk + hv + (h + 1) * dv]
        q_col = jnp.transpose(jnp.broadcast_to(q, (dk, dk)))[:, :1]
        k_col = jnp.transpose(jnp.broadcast_to(k, (dk, dk)))[:, :1]
        s_old = s_ref[h]
        qk = jnp.sum(q * k, axis=1, keepdims=True)
        o = qk * v + jnp.sum((q_col * gamma[h]) * s_old, axis=0, keepdims=True)
        s_out_ref[h] = gamma[h] * s_old + k_col * v
        res = _group_norm_gate(o, gate, gng_ref[:, h * dv:(h + 1) * dv], gnb_ref[:, h * dv:(h + 1) * dv])
        o_ref[:, h * dv:(h + 1) * dv] = res


def _ret_step(qkvg, state, layer, gn_g, gn_b, *, dk, dv):
    b, n = qkvg.shape
    hv = RET_HEADS * dv
    gamma = tuple(float(1.0 - 2.0 ** (-5.0 - h)) for h in range(RET_HEADS))
    o, s_new = pl.pallas_call(
        functools.partial(_ret_step_kernel, dk=dk, dv=dv, gamma=gamma),
        out_shape=(jax.ShapeDtypeStruct((b, 1, hv), F32),
                   jax.ShapeDtypeStruct(state.shape[1:], F32)),
        grid=(b,),
        in_specs=[
            pl.BlockSpec((None, 1, n), lambda i: (i, 0, 0)),
            pl.BlockSpec((None, None, RET_HEADS, dk, dv), lambda i: (layer, i, 0, 0, 0)),
            _resident((1, hv)), _resident((1, hv)),
        ],
        out_specs=(pl.BlockSpec((None, 1, hv), lambda i: (i, 0, 0)),
                   pl.BlockSpec((None, RET_HEADS, dk, dv), lambda i: (i, 0, 0, 0))),
        compiler_params=pltpu.CompilerParams(dimension_semantics=("arbitrary",)),
        name="ret_step",
    )(qkvg.reshape(b, 1, n), state, gn_g.reshape(1, hv), gn_b.reshape(1, hv))
    return o.reshape(b, hv), s_new


def _att_proj_kernel(x_ref, g_ref, w_ref, *out_refs, d, dh2, q_scale, head_major):
    xn = _rms(x_ref[...], g_ref[...]).astype(BF16)
    q = jnp.dot(xn, w_ref[:, 0:d], preferred_element_type=F32) * q_scale
    k = jnp.dot(xn, w_ref[:, d:2 * d], preferred_element_type=F32)
    v = jnp.dot(xn, w_ref[:, 2 * d:3 * d], preferred_element_type=F32)
    if head_major:
        k_ref, v_ref, qt_ref, kb_ref, vt_ref = out_refs
        k_ref[...] = k
        v_ref[...] = v
        for h in range(ATT_HEADS):
            qt_ref[h, 0] = jnp.transpose(q[:, h * dh2:(h + 1) * dh2]).astype(BF16)
            kb_ref[h] = k[:, h * dh2:(h + 1) * dh2].astype(BF16)
            vt_ref[h, 0] = jnp.transpose(v[:, h * dh2:(h + 1) * dh2]).astype(BF16)
    else:
        q_ref, k_ref, v_ref = out_refs
        q_ref[...] = q
        k_ref[...] = k
        v_ref[...] = v


def _att_proj(x, g, w, *, tm, head_major):
    m, d = x.shape
    dh2 = d // ATT_HEADS
    tm = min(tm, m)
    row = pl.BlockSpec((tm, d), lambda i: (i, 0))
    if head_major:
        hm = pl.BlockSpec((ATT_HEADS, tm, dh2), lambda i: (0, i, 0))
        hm_t = pl.BlockSpec((ATT_HEADS, 1, dh2, tm), lambda i: (0, i, 0, 0))
        t_shape = jax.ShapeDtypeStruct((ATT_HEADS, m // tm, dh2, tm), BF16)
        out_shape = (jax.ShapeDtypeStruct((m, d), F32),) * 2 + (
            t_shape, jax.ShapeDtypeStruct((ATT_HEADS, m, dh2), BF16), t_shape)
        out_specs = (row, row, hm_t, hm, hm_t)
    else:
        out_shape = (jax.ShapeDtypeStruct((m, d), F32),) * 3
        out_specs = (row, row, row)
    est = w.size * 2 + tm * d * 4 * 12
    return pl.pallas_call(
        functools.partial(_att_proj_kernel, d=d, dh2=dh2, q_scale=(dh2 // 2) ** -0.5 * LOG2E,
                          head_major=head_major),
        out_shape=out_shape,
        grid=(m // tm,),
        in_specs=[row, _resident((1, d)), _resident(w.shape)],
        out_specs=out_specs,
        compiler_params=pltpu.CompilerParams(
            dimension_semantics=("arbitrary",), vmem_limit_bytes=_vmem_limit(est + (8 << 20))),
        name="att_proj",
    )(x, g.reshape(1, d), w)


def _alibi_slopes():
    return np.exp2(-8.0 * (np.arange(ATT_HEADS, dtype=np.float64) + 1.0) / ATT_HEADS)


def _lambda(lq1_ref, lk1_ref, lq2_ref, lk2_ref, lam_init):
    a = jnp.sum(lq1_ref[...] * lk1_ref[...], axis=1, keepdims=True)
    b = jnp.sum(lq2_ref[...] * lk2_ref[...], axis=1, keepdims=True)
    return jnp.exp(a) - jnp.exp(b) + lam_init


def _sub_norm(o, sub_g, lam_init):
    return _rms(o, sub_g) * (1.0 - lam_init)


def _flash_kernel(lq1_ref, lk1_ref, lq2_ref, lk2_ref, subg_ref, slope_ref, kfeat_ref, mask_ref, qt_ref, k_ref, vt_ref,
                  o_ref, qz_ref, sa_ref, sb_ref, m_ref, acc_ref, *, tq, tk, dh, lam_init):
    i = pl.program_id(1)
    dv = 2 * dh
    dh2 = 2 * dh
    ratio = tq // tk
    slope2 = slope_ref[:, :1] * LOG2E

    qt = jnp.concatenate([qt_ref[b] for b in range(qt_ref.shape[0])], axis=1)
    feat = lax.broadcasted_iota(jnp.int32, qt.shape, 0)
    zero = jnp.zeros_like(qt)
    qz_ref[:dh2, :tq] = jnp.where(feat < dh, qt, zero)
    qz_ref[:dh2, tq:] = jnp.where(feat >= dh, qt, zero)
    bias_row = lax.broadcasted_iota(jnp.int32, (LANES, 2 * tq), 0) < N_BIAS_FEATURES
    qz_ref[dh2:, :] = jnp.where(bias_row, 1.0, 0.0).astype(BF16)
    m_ref[...] = jnp.full_like(m_ref, NEG_BIG)
    acc_ref[...] = jnp.zeros_like(acc_ref)
    ones = jnp.ones((BF16_SUBLANES, tk), BF16)
    kfeat = kfeat_ref[...]

    def scores(j, s_ref):
        kt = k_ref[pl.ds(pl.multiple_of(j * tk, tk), tk), :]
        s_ref[...] = jnp.dot(jnp.concatenate([kt, kfeat], axis=1), qz_ref[...],
                             preferred_element_type=F32)

    def consume(j, s_ref, diag_tile=None):
        va = jnp.concatenate([vt_ref[j], ones], axis=0)
        blk = jnp.full((1, 1), j * tk - i * tq, jnp.int32).astype(F32) * slope2
        s = s_ref[...]
        if diag_tile is not None:
            mask = mask_ref[diag_tile]
            s = s + jnp.concatenate([mask, mask], axis=1)
        m_old = m_ref[...]
        m_new = jnp.maximum(m_old, jnp.max(s, axis=0, keepdims=True) + blk)
        p = jnp.exp2(s - (m_new - blk))
        alpha = jnp.exp2(m_old - m_new)
        acc_ref[...] = alpha * acc_ref[...] + jnp.dot(va, p.astype(BF16), preferred_element_type=F32)
        m_ref[...] = m_new

    scores(0, sa_ref)

    def pair(t, carry):
        a = 2 * t
        scores(a + 1, sb_ref)
        consume(a, sa_ref)
        scores(a + 2, sa_ref)
        consume(a + 1, sb_ref)
        return carry

    first_diag = ratio * i
    lax.fori_loop(0, first_diag // 2, pair, 0)
    for d in range(0, ratio, 2):
        scores(first_diag + d + 1, sb_ref)
        consume(first_diag + d, sa_ref, diag_tile=d)
        if d + 2 < ratio:
            scores(first_diag + d + 2, sa_ref)
        consume(first_diag + d + 1, sb_ref, diag_tile=d + 1)

    lam = _lambda(lq1_ref, lk1_ref, lq2_ref, lk2_ref, lam_init)
    a1 = acc_ref[:, :tq]
    a2 = acc_ref[:, tq:]
    o = a1[:dv] / a1[dv:dv + 1] - lam * (a2[:dv] / a2[dv:dv + 1])
    on = o * lax.rsqrt(jnp.mean(o * o, axis=0, keepdims=True) + NORM_EPS) * subg_ref[...] * (1.0 - lam_init)
    o_ref[...] = jnp.transpose(on).astype(o_ref.dtype)


def _bf16_split(x, n_parts):
    parts = []
    rest = np.asarray(x, np.float32)
    for _ in range(n_parts):
        part = rest.astype(BF16).astype(np.float32)
        parts.append(part)
        rest = rest - part
    return parts


def _flash(qt, kb, vt, lams, sub_g, *, tq, lam_init):
    h, n_kt, dh2, tk = vt.shape
    t_len = kb.shape[1]
    dh = dh2 // 2
    tq = max(min(tq, t_len), 2 * tk)
    ratio = tq // tk
    assert tq % tk == 0 and ratio % 2 == 0 and t_len % tq == 0
    slopes_np = _alibi_slopes()
    slopes = jnp.asarray(np.broadcast_to(slopes_np[:, None, None], (h, 1, LANES)), F32)
    local = (np.arange(tk, dtype=np.float64)[None, :] * (slopes_np * LOG2E)[:, None]).astype(np.float32)
    kfeat_np = np.zeros((h, tk, LANES), np.float32)
    for col, part in enumerate(_bf16_split(local, N_BIAS_FEATURES)):
        kfeat_np[:, :, col] = part
    kfeat = jnp.asarray(kfeat_np, BF16)
    key = np.arange(tk)[None, :, None] + tk * np.arange(ratio)[:, None, None]
    masks = jnp.asarray(np.where(key <= np.arange(tq)[None, None, :], 0.0, NEG_BIG), F32)
    lam_specs = [_resident((1, dh))] * 4
    est = (2 * 2 * t_len * dh2 * 2 + ratio * tq * tk * 4 + 2 * tq * tk * 4 * 5
           + 2 * tq * 2 * dh2 * 4 * 3 + 4 * tq * dh2 * 2 * 2)
    return pl.pallas_call(
        functools.partial(_flash_kernel, tq=tq, tk=tk, dh=dh, lam_init=lam_init),
        out_shape=jax.ShapeDtypeStruct((t_len, h * dh2), BF16),
        grid=(h, t_len // tq),
        in_specs=lam_specs + [
            _resident((dh2, 1)),
            pl.BlockSpec((None, 1, LANES), lambda hh, i: (hh, 0, 0)),
            pl.BlockSpec((None, tk, LANES), lambda hh, i: (hh, 0, 0)),
            _resident(masks.shape),
            pl.BlockSpec((None, ratio, dh2, tk), lambda hh, i: (hh, i, 0, 0)),
            pl.BlockSpec((None, t_len, dh2), lambda hh, i: (hh, 0, 0)),
            pl.BlockSpec((None, n_kt, dh2, tk), lambda hh, i: (hh, 0, 0, 0)),
        ],
        out_specs=pl.BlockSpec((tq, dh2), lambda hh, i: (i, hh)),
        scratch_shapes=[pltpu.VMEM((dh2 + LANES, 2 * tq), BF16),
                        pltpu.VMEM((tk, 2 * tq), F32), pltpu.VMEM((tk, 2 * tq), F32),
                        pltpu.VMEM((1, 2 * tq), F32), pltpu.VMEM((dh2 + BF16_SUBLANES, 2 * tq), F32)],
        compiler_params=pltpu.CompilerParams(
            dimension_semantics=("arbitrary", "arbitrary"), vmem_limit_bytes=_vmem_limit(est + (8 << 20))),
        name="flash_diff_attn",
    )(*[a.reshape(1, dh) for a in lams], sub_g.reshape(dh2, 1), slopes, kfeat, masks, qt, kb, vt)


def _dec_attn_kernel(pt_ref, lq1_ref, lk1_ref, lq2_ref, lk2_ref, subg_ref, slope_ref, feat_ref, hs_ref,
                     q_ref, kn_ref, vn_ref, *rest, n_pg, page, past_len, dh, lam_init):
    del pt_ref
    k_refs = rest[:n_pg]
    v_refs = rest[n_pg:2 * n_pg]
    o_ref, m_ref, l_ref, acc_ref, sc_ref = rest[2 * n_pg:]
    g = pl.program_id(1)
    dh2 = 2 * dh
    nh = q_ref.shape[0]

    @pl.when(g == 0)
    def _():
        m_ref[...] = jnp.full_like(m_ref, NEG_BIG)
        l_ref[...] = jnp.zeros_like(l_ref)
        acc_ref[...] = jnp.zeros_like(acc_ref)

    q = q_ref[...]
    slope2 = slope_ref[...] * LOG2E
    feat = feat_ref[...]
    hs = hs_ref[...]

    def over_positions(x, op):
        n = x.shape[0]
        if n % POSITION_CHAINS == 0 and n > POSITION_CHAINS:
            x = op(x.reshape(POSITION_CHAINS, n // POSITION_CHAINS, nh, x.shape[-1]), axis=1)
        return op(x, axis=0)

    def update(ss, offs, vs):
        m_old = m_ref[...]
        m_new = m_old
        for s, off in zip(ss, offs):
            m_new = jnp.maximum(m_new, over_positions(s, jnp.max) + off)
        l_add = jnp.zeros_like(m_old)
        a_add = jnp.zeros_like(m_old)
        b_add = jnp.zeros_like(m_old)
        for s, off, v in zip(ss, offs, vs):
            p = jnp.exp2(s - (m_new - off)[None])
            l_add = l_add + over_positions(p, jnp.sum)
            a_add = a_add + over_positions(p * v, jnp.sum)
            b_add = b_add + over_positions(p * pltpu.roll(v, dh, 2), jnp.sum)
        alpha = jnp.exp2(m_old - m_new)
        l_ref[...] = alpha * l_ref[...] + l_add
        acc_ref[0] = alpha * acc_ref[0] + a_add
        acc_ref[1] = alpha * acc_ref[1] + b_add
        m_ref[...] = m_new

    offs = []
    for pi in range(n_pg):
        kp = k_refs[pi][...]
        prod = (kp * q[None]).astype(BF16).reshape(page * nh, dh2)
        sc_ref[pi] = jnp.dot(jnp.concatenate([prod, feat], axis=1), hs, preferred_element_type=F32)
        first_pos = (g * n_pg + pi) * page - past_len
        offs.append(jnp.full((1, 1), first_pos, jnp.int32).astype(F32) * slope2)
    for g0 in range(0, n_pg, DECODE_PAGES_PER_UPDATE):
        pages = range(g0, g0 + DECODE_PAGES_PER_UPDATE)
        update([sc_ref[pi].reshape(page, nh, LANES) for pi in pages], [offs[pi] for pi in pages],
               [v_refs[pi][...] for pi in pages])

    @pl.when(g == pl.num_programs(1) - 1)
    def _():
        kn = kn_ref[...]
        vn = vn_ref[...]
        s_new = jnp.dot((kn * q).astype(BF16), hs[:dh2], preferred_element_type=F32)
        update([s_new[None]], [jnp.zeros_like(slope2)], [vn[None]])
        lam = _lambda(lq1_ref, lk1_ref, lq2_ref, lk2_ref, lam_init)
        first_half = lax.broadcasted_iota(jnp.int32, (nh, LANES), 1) < dh

        def swap(x):
            return pltpu.roll(x, dh, 1)

        l = l_ref[...]
        l1 = jnp.where(first_half, l, swap(l))
        l2 = jnp.where(first_half, swap(l), l)
        a = acc_ref[0]
        b = acc_ref[1]
        o1 = jnp.where(first_half, a, swap(b))
        o2 = jnp.where(first_half, swap(b), a)
        o = o1 / l1 - lam * (o2 / l2)
        o_ref[...] = _sub_norm(o, subg_ref[...], lam_init)


def _dec_attn(q, kn, vn, cache_k, cache_v, layer, page_table, lams, sub_g, *, lam_init, n_pg):
    b, nh, dh2 = q.shape
    dh = dh2 // 2
    n_pages = page_table.shape[1]
    page = cache_k.shape[2]
    while n_pages % n_pg:
        n_pg //= 2
    slopes = jnp.asarray(np.broadcast_to(_alibi_slopes()[:, None], (nh, LANES)), F32)
    local = (np.arange(page, dtype=np.float64)[:, None] * (_alibi_slopes() * LOG2E)[None, :]).astype(np.float32)
    feat_np = np.zeros((page * nh, LANES), np.float32)
    for col, part in enumerate(_bf16_split(local, N_BIAS_FEATURES)):
        feat_np[:, col] = part.reshape(-1)
    feat = jnp.asarray(feat_np, BF16)
    hs_np = np.zeros((dh2 + LANES, LANES), np.float32)
    hs_np[:dh, :dh] = 1.0
    hs_np[dh:dh2, dh:] = 1.0
    hs_np[dh2:dh2 + N_BIAS_FEATURES, :] = 1.0
    hs = jnp.asarray(hs_np, BF16)

    def page_spec(pi):
        return pl.BlockSpec((None, None, page, nh, dh2),
                            lambda bb, g, pt: (layer, pt[bb, g * n_pg + pi], 0, 0, 0))

    tok = pl.BlockSpec((None, nh, dh2), lambda bb, g, pt: (bb, 0, 0))

    def const(shape):
        zeros = (0,) * len(shape)
        return pl.BlockSpec(shape, lambda bb, g, pt: zeros)

    grid_spec = pltpu.PrefetchScalarGridSpec(
        num_scalar_prefetch=1,
        grid=(b, n_pages // n_pg),
        in_specs=[const((1, dh))] * 4 + [const((1, dh2)), const((nh, LANES)), const(feat.shape), const(hs.shape),
                                         tok, tok, tok]
                 + [page_spec(pi) for pi in range(n_pg)] * 2,
        out_specs=tok,
        scratch_shapes=[pltpu.VMEM((nh, LANES), F32), pltpu.VMEM((nh, LANES), F32),
                        pltpu.VMEM((2, nh, dh2), F32), pltpu.VMEM((n_pg, page * nh, LANES), F32)],
    )
    est = 2 * 2 * n_pg * page * nh * dh2 * 4 + n_pg * page * nh * LANES * 4 + page * nh * LANES * 4 * 12
    return pl.pallas_call(
        functools.partial(_dec_attn_kernel, n_pg=n_pg, page=page, past_len=n_pages * page, dh=dh,
                          lam_init=lam_init),
        out_shape=jax.ShapeDtypeStruct((b, nh, dh2), F32),
        grid_spec=grid_spec,
        compiler_params=pltpu.CompilerParams(
            dimension_semantics=("arbitrary", "arbitrary"), vmem_limit_bytes=_vmem_limit(est + (8 << 20))),
        name="decode_diff_attn",
    )(page_table, *[a.reshape(1, dh) for a in lams], sub_g.reshape(1, dh2), slopes, feat, hs, q, kn, vn,
      *([cache_k] * n_pg), *([cache_v] * n_pg))


FFN_CHUNK = MXU_TILE
ROW_TILE = 512
FLASH_Q_TILE = 1024
FLASH_K_TILE = 512
DECODE_PAGES_PER_STEP = 8
DECODE_PAGES_PER_UPDATE = 2
POSITION_CHAINS = 8


def _prep_ffn_weights(w_gu, w_d):
    assert w_d.shape[0] % FFN_CHUNK == 0
    return w_gu.astype(BF16), w_d.astype(BF16)


def kernel(x_prompt, x_sample, state_ret, cache_k, cache_v, page_table, norm_g, ffn_w_gu, ffn_w_d, final_g,
           ret_w_in, ret_w_out, ret_gn_g, ret_gn_b, att_w_in, att_w_out, att_lam_q1, att_lam_k1, att_lam_q2,
           att_lam_k2, att_subln_g):
    depth = norm_g.shape[0]
    batch, seq, d_model = x_prompt.shape
    dec_batch, dec_seq, _ = x_sample.shape
    assert batch == 1 and dec_seq == 1
    ret_dk = d_model // RET_HEADS
    ret_dv = 2 * d_model // RET_HEADS
    att_dh2 = d_model // ATT_HEADS

    yp = x_prompt.reshape(seq, d_model)
    ys = x_sample.reshape(dec_batch, d_model)
    ret_p, ret_s, kp_rows, vp_rows, ks_rows, vs_rows = [], [], [], [], [], []
    for i in range(depth):
        r = i // N_MIXERS
        last = i == depth - 1
        wgu0, wd0 = _prep_ffn_weights(ffn_w_gu[i, 0], ffn_w_d[i, 0])
        wgu1, wd1 = _prep_ffn_weights(ffn_w_gu[i, 1], ffn_w_d[i, 1])
        yp = _ffn(yp, norm_g[i, 0], wgu0, wd0, final_g, tm=ROW_TILE, final=False)
        ys = _ffn(ys, norm_g[i, 0], wgu0, wd0, final_g, tm=ROW_TILE, final=False)
        if i % N_MIXERS == 0:
            w_in = ret_w_in[r].astype(BF16)
            w_out = ret_w_out[r].astype(BF16)
            k_scale = ret_dk ** -0.5
            qkvg_p = _ret_proj(yp, norm_g[i, 1], w_in, tm=ROW_TILE, out_dtype=BF16, k_scale=k_scale)
            qkvg_s = _ret_proj(ys, norm_g[i, 1], w_in, tm=ROW_TILE, out_dtype=F32, k_scale=k_scale)
            op, sp = _ret_scan(qkvg_p, ret_gn_g[r], ret_gn_b[r], dk=ret_dk, dv=ret_dv)
            osm, ss = _ret_step(qkvg_s, state_ret, r, ret_gn_g[r], ret_gn_b[r], dk=ret_dk, dv=ret_dv)
            ret_p.append(sp.reshape(1, RET_HEADS, ret_dk, ret_dv))
            ret_s.append(ss)
        else:
            w_in = att_w_in[r].astype(BF16)
            w_out = att_w_out[r].astype(BF16)
            lam_init = 0.8 - 0.6 * math.exp(-0.3 * i)
            lams = (att_lam_q1[r], att_lam_k1[r], att_lam_q2[r], att_lam_k2[r])
            kp, vp, qt, kb, vt = _att_proj(yp, norm_g[i, 1], w_in, tm=FLASH_K_TILE, head_major=True)
            qs, ksm, vsm = _att_proj(ys, norm_g[i, 1], w_in, tm=ROW_TILE, head_major=False)
            op = _flash(qt, kb, vt, lams, att_subln_g[r], tq=FLASH_Q_TILE, lam_init=lam_init)
            hs = (dec_batch, ATT_HEADS, att_dh2)
            osm = _dec_attn(qs.reshape(hs), ksm.reshape(hs), vsm.reshape(hs), cache_k, cache_v, r, page_table,
                            lams, att_subln_g[r], lam_init=lam_init, n_pg=DECODE_PAGES_PER_STEP)
            osm = osm.reshape(dec_batch, d_model)
            kp_rows.append(kp.reshape(1, seq, ATT_HEADS, att_dh2))
            vp_rows.append(vp.reshape(1, seq, ATT_HEADS, att_dh2))
            ks_rows.append(ksm.reshape(dec_batch, 1, ATT_HEADS, att_dh2))
            vs_rows.append(vsm.reshape(dec_batch, 1, ATT_HEADS, att_dh2))
        yp = _out_proj(yp, op, w_out, tm=ROW_TILE)
        ys = _out_proj(ys, osm, w_out, tm=ROW_TILE)
        yp = _ffn(yp, norm_g[i, 2], wgu1, wd1, final_g, tm=ROW_TILE, final=last)
        ys = _ffn(ys, norm_g[i, 2], wgu1, wd1, final_g, tm=ROW_TILE, final=last)
    return (yp.reshape(1, seq, d_model), ys.reshape(dec_batch, 1, d_model),
            jnp.stack(ret_p), jnp.stack(kp_rows), jnp.stack(vp_rows),
            jnp.stack(ret_s), jnp.stack(ks_rows), jnp.stack(vs_rows))
```

```python
import functools
import math

import numpy as np
import jax
import jax.numpy as jnp
from jax import lax
from jax.experimental import pallas as pl
from jax.experimental.pallas import tpu as pltpu

F32 = jnp.float32
BF16 = jnp.bfloat16

NORM_EPS = 1e-6
GN_EPS = 1e-5
NEG_BIG = -1e30
LOG2E = 1.4426950408889634
N_MIXERS = 2

RET_HEADS = 8
RET_CHUNK = 128
RET_CHUNKS_PER_STEP = 4
RET_SEQS_PER_STEP = 4
ATT_HEADS = 8

V7X_VMEM_BYTES = 64 * 1024 * 1024
MXU_TILE = 256
LANES = 128
BF16_SUBLANES = 16
F32_SUBLANES = 8
N_BIAS_FEATURES = 3


def _vmem_limit(nbytes):
    return int(min(V7X_VMEM_BYTES - 8 * 1024 * 1024, max(32 * 1024 * 1024, nbytes)))


def _rms(x, g):
    return x * lax.rsqrt(jnp.mean(x * x, axis=-1, keepdims=True) + NORM_EPS) * g


def _resident(shape):
    zeros = (0,) * len(shape)
    return pl.BlockSpec(shape, lambda *_: zeros, pipeline_mode=pl.Buffered(1))


def _ffn_kernel(x_ref, *refs, d_ff, ck, final, mixed):
    if mixed:
        mix_ref, wo_ref, wgu_ref, wd_ref, g_ref, fg_ref, o_ref, xn_ref, acc_ref = refs
        x = x_ref[...] + jnp.dot(mix_ref[...].astype(BF16), wo_ref[...], preferred_element_type=F32)
    else:
        wgu_ref, wd_ref, g_ref, fg_ref, o_ref, xn_ref, acc_ref = refs
        x = x_ref[...]
    xn_ref[...] = _rms(x, g_ref[...]).astype(BF16)
    acc_ref[...] = jnp.zeros_like(acc_ref)
    for c0 in range(0, d_ff, ck):
        xn = xn_ref[...]
        gate = jnp.dot(xn, wgu_ref[:, c0:c0 + ck], preferred_element_type=F32)
        up = jnp.dot(xn, wgu_ref[:, d_ff + c0:d_ff + c0 + ck], preferred_element_type=F32)
        h = (jax.nn.silu(gate) * up).astype(BF16)
        acc_ref[...] += jnp.dot(h, wd_ref[c0:c0 + ck, :], preferred_element_type=F32)
    y = x + 0.5 * acc_ref[...]
    if final:
        y = _rms(y, fg_ref[...])
    o_ref[...] = y


def _ffn(x, g, wgu_r, wd_r, final_g, *, tm, final, mix=None, w_out=None):
    m, d = x.shape
    d_ff = wd_r.shape[0]
    ck = FFN_CHUNK
    tm = min(tm, m)
    mixed = mix is not None
    row = pl.BlockSpec((tm, d), lambda i: (i, 0))
    est = (wgu_r.size + wd_r.size) * 2 + tm * d * (4 * 4 + 2 + 4) + tm * 2 * ck * 4 * 3
    mix_specs, mix_args = [], []
    if mixed:
        kdim = mix.shape[1]
        mix_specs = [pl.BlockSpec((tm, kdim), lambda i: (i, 0)), _resident(w_out.shape)]
        mix_args = [mix, w_out]
        est += w_out.size * 2 + tm * kdim * 4 * 2 + tm * d * 4
    return pl.pallas_call(
        functools.partial(_ffn_kernel, d_ff=d_ff, ck=ck, final=final, mixed=mixed),
        out_shape=jax.ShapeDtypeStruct((m, d), F32),
        grid=(m // tm,),
        in_specs=[row] + mix_specs + [_resident(wgu_r.shape), _resident(wd_r.shape),
                                      _resident((1, d)), _resident((1, d))],
        out_specs=row,
        scratch_shapes=[pltpu.VMEM((tm, d), BF16), pltpu.VMEM((tm, d), F32)],
        compiler_params=pltpu.CompilerParams(
            dimension_semantics=("arbitrary",), vmem_limit_bytes=_vmem_limit(est + (8 << 20))),
        name="ffn",
    )(x, *mix_args, wgu_r, wd_r, g.reshape(1, d), final_g.reshape(1, d))


def _ret_proj_kernel(x_ref, g_ref, w_ref, o_ref, *, d, k_scale):
    xn = _rms(x_ref[...], g_ref[...]).astype(BF16)
    n = w_ref.shape[1]
    for c in range(n // d):
        r = jnp.dot(xn, w_ref[:, c * d:(c + 1) * d], preferred_element_type=F32)
        if c == 1:
            r = r * k_scale
        o_ref[:, c * d:(c + 1) * d] = r.astype(o_ref.dtype)


def _ret_proj(x, g, w, *, tm, out_dtype, k_scale):
    m, d = x.shape
    n = w.shape[1]
    tm = min(tm, m)
    est = w.size * 2 + tm * d * 4 * 2 + tm * n * 4 * 2 + tm * d * 4 * 3
    return pl.pallas_call(
        functools.partial(_ret_proj_kernel, d=d, k_scale=k_scale),
        out_shape=jax.ShapeDtypeStruct((m, n), out_dtype),
        grid=(m // tm,),
        in_specs=[pl.BlockSpec((tm, d), lambda i: (i, 0)), _resident((1, d)), _resident(w.shape)],
        out_specs=pl.BlockSpec((tm, n), lambda i: (i, 0)),
        compiler_params=pltpu.CompilerParams(
            dimension_semantics=("arbitrary",), vmem_limit_bytes=_vmem_limit(est + (8 << 20))),
        name="ret_proj",
    )(x, g.reshape(1, d), w)


def _ret_consts(chunk):
    lg = np.log1p(-np.exp2(-5.0 - np.arange(RET_HEADS, dtype=np.float64)))
    pos = np.arange(chunk, dtype=np.float64)
    n = pos[:, None] - pos[None, :]
    decay = np.where(n[None] >= 0, np.exp(lg[:, None, None] * np.maximum(n, 0.0)[None]), 0.0)
    q_dec = np.exp(lg[:, None] * (pos[None, :] + 1.0))
    k_dec = np.exp(lg[:, None] * (chunk - 1.0 - pos)[None, :])
    s_dec = np.exp(lg * chunk)
    return decay, q_dec, k_dec, s_dec


def _group_norm_gate(o, gate, gn_g, gn_b):
    mu = jnp.mean(o, axis=-1, keepdims=True)
    var = jnp.mean(jnp.square(o - mu), axis=-1, keepdims=True)
    on = (o - mu) * lax.rsqrt(var + GN_EPS)
    on = on * gn_g + gn_b
    return on * jax.nn.silu(gate)


def _ret_scan_kernel(q_ref, k_ref, v_ref, gt_ref, dec_ref, qd_ref, kd_ref, gng_ref, gnb_ref,
                     o_ref, s_out_ref, s_ref, *, dk, dv, s_dec):
    t = pl.program_id(0)

    @pl.when(t == 0)
    def _():
        s_ref[...] = jnp.zeros_like(s_ref)

    c = dec_ref.shape[1]
    for r0 in range(0, q_ref.shape[0], c):
        rows = slice(r0, r0 + c)
        for h in range(RET_HEADS):
            qh = q_ref[rows, h * dk:(h + 1) * dk]
            kh = k_ref[rows, h * dk:(h + 1) * dk]
            vh = v_ref[rows, h * dv:(h + 1) * dv]
            sc = lax.dot_general(qh, kh, (((1,), (1,)), ((), ())), preferred_element_type=F32) * dec_ref[h]
            o = jnp.dot(sc.astype(BF16), vh, preferred_element_type=F32)
            s_old = s_ref[h]
            qd = (qh.astype(F32) * qd_ref[h]).astype(BF16)
            o = o + jnp.dot(qd, s_old.astype(BF16), preferred_element_type=F32)
            kd = (kh.astype(F32) * kd_ref[h]).astype(BF16)
            s_ref[h] = s_dec[h] * s_old + lax.dot_general(kd, vh, (((0,), (0,)), ((), ())),
                                                          preferred_element_type=F32)
            gate = gt_ref[rows, h * dv:(h + 1) * dv].astype(F32)
            res = _group_norm_gate(o, gate, gng_ref[:, h * dv:(h + 1) * dv], gnb_ref[:, h * dv:(h + 1) * dv])
            o_ref[rows, h * dv:(h + 1) * dv] = res.astype(o_ref.dtype)

    @pl.when(t == pl.num_programs(0) - 1)
    def _():
        s_out_ref[...] = s_ref[...]


def _ret_scan(qkvg, gn_g, gn_b, *, dk, dv):
    t_len = qkvg.shape[0]
    c = RET_CHUNK if t_len % RET_CHUNK == 0 else t_len
    hk, hv = RET_HEADS * dk, RET_HEADS * dv
    decay, q_dec, k_dec, s_dec = _ret_consts(c)
    dec = jnp.asarray(decay, F32)
    qd = jnp.asarray(np.broadcast_to(q_dec[:, :, None], (RET_HEADS, c, dk)), F32)
    kd = jnp.asarray(np.broadcast_to(k_dec[:, :, None], (RET_HEADS, c, dk)), F32)
    rows = c * RET_CHUNKS_PER_STEP if t_len % (c * RET_CHUNKS_PER_STEP) == 0 else c
    return pl.pallas_call(
        functools.partial(_ret_scan_kernel, dk=dk, dv=dv, s_dec=tuple(float(s) for s in s_dec)),
        out_shape=(jax.ShapeDtypeStruct((t_len, hv), BF16),
                   jax.ShapeDtypeStruct((RET_HEADS, dk, dv), F32)),
        grid=(t_len // rows,),
        in_specs=[
            pl.BlockSpec((rows, hk), lambda t: (t, 0)),
            pl.BlockSpec((rows, hk), lambda t: (t, 1)),
            pl.BlockSpec((rows, hv), lambda t: (t, 1)),
            pl.BlockSpec((rows, hv), lambda t: (t, 2)),
            _resident(dec.shape), _resident(qd.shape), _resident(kd.shape),
            _resident((1, hv)), _resident((1, hv)),
        ],
        out_specs=(pl.BlockSpec((rows, hv), lambda t: (t, 0)),
                   pl.BlockSpec((RET_HEADS, dk, dv), lambda t: (0, 0, 0))),
        scratch_shapes=[pltpu.VMEM((RET_HEADS, dk, dv), F32)],
        compiler_params=pltpu.CompilerParams(dimension_semantics=("arbitrary",)),
        name="ret_scan",
    )(qkvg, qkvg, qkvg, qkvg, dec, qd, kd, gn_g.reshape(1, hv), gn_b.reshape(1, hv))


def _ret_step_kernel(x_ref, s_ref, gng_ref, gnb_ref, o_ref, s_out_ref, *, dk, dv, gamma):
    nh = RET_HEADS
    per_head = dv // LANES

    def head_row(x, base, h):
        r0 = base + per_head * h
        return jnp.concatenate([x[r0 + j:r0 + j + 1] for j in range(per_head)], axis=1)

    for b in range(x_ref.shape[0]):
        x = x_ref[b]
        qk_cols = jnp.transpose(jnp.concatenate([x[:2 * nh], jnp.zeros((dk - 2 * nh, dk), F32)], axis=0))
        for h in range(nh):
            q = x[h:h + 1]
            k = x[nh + h:nh + h + 1]
            v = head_row(x, 2 * nh, h)
            gate = head_row(x, 2 * nh + per_head * nh, h)
            q_col = qk_cols[:, h:h + 1]
            k_col = qk_cols[:, nh + h:nh + h + 1]
            s_old = s_ref[b, h]
            qk = jnp.sum(q * k, axis=1, keepdims=True)
            o = qk * v + jnp.sum((q_col * gamma[h]) * s_old, axis=0, keepdims=True)
            s_out_ref[b, h] = gamma[h] * s_old + k_col * v
            res = _group_norm_gate(o, gate, gng_ref[:, h * dv:(h + 1) * dv], gnb_ref[:, h * dv:(h + 1) * dv])
            o_ref[b, :, h * dv:(h + 1) * dv] = res


def _ret_step(qkvg, state, layer, gn_g, gn_b, *, dk, dv):
    b, n = qkvg.shape
    hv = RET_HEADS * dv
    assert dk == LANES and dv % LANES == 0
    nb = RET_SEQS_PER_STEP if b % RET_SEQS_PER_STEP == 0 else 1
    gamma = tuple(float(1.0 - 2.0 ** (-5.0 - h)) for h in range(RET_HEADS))
    o, s_new = pl.pallas_call(
        functools.partial(_ret_step_kernel, dk=dk, dv=dv, gamma=gamma),
        out_shape=(jax.ShapeDtypeStruct((b, 1, hv), F32),
                   jax.ShapeDtypeStruct(state.shape[1:], F32)),
        grid=(b // nb,),
        in_specs=[
            pl.BlockSpec((nb, n // LANES, LANES), lambda i: (i, 0, 0)),
            pl.BlockSpec((None, nb, RET_HEADS, dk, dv), lambda i: (layer, i, 0, 0, 0)),
            _resident((1, hv)), _resident((1, hv)),
        ],
        out_specs=(pl.BlockSpec((nb, 1, hv), lambda i: (i, 0, 0)),
                   pl.BlockSpec((nb, RET_HEADS, dk, dv), lambda i: (i, 0, 0, 0))),
        compiler_params=pltpu.CompilerParams(dimension_semantics=("arbitrary",)),
        name="ret_step",
    )(qkvg.reshape(b, n // LANES, LANES), state, gn_g.reshape(1, hv), gn_b.reshape(1, hv))
    return o.reshape(b, hv), s_new


def _att_proj_kernel(x_ref, g_ref, w_ref, *out_refs, d, dh2, q_scale, head_major):
    xn = _rms(x_ref[...], g_ref[...]).astype(BF16)
    q = jnp.dot(xn, w_ref[:, 0:d], preferred_element_type=F32) * q_scale
    k = jnp.dot(xn, w_ref[:, d:2 * d], preferred_element_type=F32)
    v = jnp.dot(xn, w_ref[:, 2 * d:3 * d], preferred_element_type=F32)
    if head_major:
        k_ref, v_ref, qt_ref, kb_ref, vt_ref = out_refs
        k_ref[...] = k
        v_ref[...] = v
        for h in range(ATT_HEADS):
            qt_ref[h, 0] = jnp.transpose(q[:, h * dh2:(h + 1) * dh2]).astype(BF16)
            kb_ref[h] = k[:, h * dh2:(h + 1) * dh2].astype(BF16)
            vt_ref[h, 0] = jnp.transpose(v[:, h * dh2:(h + 1) * dh2]).astype(BF16)
    else:
        q_ref, k_ref, v_ref = out_refs
        q_ref[...] = q
        k_ref[...] = k
        v_ref[...] = v


def _att_proj(x, g, w, *, tm, head_major):
    m, d = x.shape
    dh2 = d // ATT_HEADS
    tm = min(tm, m)
    row = pl.BlockSpec((tm, d), lambda i: (i, 0))
    if head_major:
        hm = pl.BlockSpec((ATT_HEADS, tm, dh2), lambda i: (0, i, 0))
        hm_t = pl.BlockSpec((ATT_HEADS, 1, dh2, tm), lambda i: (0, i, 0, 0))
        t_shape = jax.ShapeDtypeStruct((ATT_HEADS, m // tm, dh2, tm), BF16)
        out_shape = (jax.ShapeDtypeStruct((m, d), F32),) * 2 + (
            t_shape, jax.ShapeDtypeStruct((ATT_HEADS, m, dh2), BF16), t_shape)
        out_specs = (row, row, hm_t, hm, hm_t)
    else:
        out_shape = (jax.ShapeDtypeStruct((m, d), F32),) * 3
        out_specs = (row, row, row)
    est = w.size * 2 + tm * d * 4 * 12
    return pl.pallas_call(
        functools.partial(_att_proj_kernel, d=d, dh2=dh2, q_scale=(dh2 // 2) ** -0.5 * LOG2E,
                          head_major=head_major),
        out_shape=out_shape,
        grid=(m // tm,),
        in_specs=[row, _resident((1, d)), _resident(w.shape)],
        out_specs=out_specs,
        compiler_params=pltpu.CompilerParams(
            dimension_semantics=("arbitrary",), vmem_limit_bytes=_vmem_limit(est + (8 << 20))),
        name="att_proj",
    )(x, g.reshape(1, d), w)


def _alibi_slopes():
    return np.exp2(-8.0 * (np.arange(ATT_HEADS, dtype=np.float64) + 1.0) / ATT_HEADS)


def _lambda(lq1_ref, lk1_ref, lq2_ref, lk2_ref, lam_init):
    a = jnp.sum(lq1_ref[...] * lk1_ref[...], axis=1, keepdims=True)
    b = jnp.sum(lq2_ref[...] * lk2_ref[...], axis=1, keepdims=True)
    return jnp.exp(a) - jnp.exp(b) + lam_init


def _sub_norm(o, sub_g, lam_init):
    return _rms(o, sub_g) * (1.0 - lam_init)


def _flash_kernel(mask_ref, k_ref, vt_ref, qt_ref, kfeat_ref, subg_ref, slope_ref, lq1_ref, lk1_ref, lq2_ref, lk2_ref,
                  o_ref, qz_ref, sa_ref, sb_ref, acc_ref, m_ref, *, tq, tk, dh, lam_init):
    i = pl.program_id(1)
    dv = 2 * dh
    dh2 = 2 * dh
    ratio = tq // tk
    slope2 = slope_ref[:, :1] * LOG2E

    qt = jnp.concatenate([qt_ref[b] for b in range(qt_ref.shape[0])], axis=1)
    feat = lax.broadcasted_iota(jnp.int32, qt.shape, 0)
    zero = jnp.zeros_like(qt)
    qz_ref[:dh2, :tq] = jnp.where(feat < dh, qt, zero)
    qz_ref[:dh2, tq:] = jnp.where(feat >= dh, qt, zero)
    bias_row = lax.broadcasted_iota(jnp.int32, (LANES, 2 * tq), 0) < N_BIAS_FEATURES
    qz_ref[dh2:, :] = jnp.where(bias_row, 1.0, 0.0).astype(BF16)
    m_ref[...] = jnp.full_like(m_ref, NEG_BIG)
    acc_ref[...] = jnp.zeros_like(acc_ref)
    ones = jnp.ones((BF16_SUBLANES, tk), BF16)
    kfeat = kfeat_ref[...]

    def scores(j, s_ref):
        kt = k_ref[pl.ds(pl.multiple_of(j * tk, tk), tk), :]
        s_ref[...] = jnp.dot(jnp.concatenate([kt, kfeat], axis=1), qz_ref[...],
                             preferred_element_type=F32)

    def consume(j, s_ref, diag_tile=None):
        va = jnp.concatenate([vt_ref[j], ones], axis=0)
        blk = jnp.full((1, 1), j * tk - i * tq, jnp.int32).astype(F32) * slope2
        s = s_ref[...]
        if diag_tile is not None:
            mask = mask_ref[diag_tile]
            s = s + jnp.concatenate([mask, mask], axis=1)
        m_old = m_ref[0:1]
        m_new = jnp.maximum(m_old, jnp.max(s, axis=0, keepdims=True) + blk)
        p = jnp.exp2(s - (m_new - blk))
        alpha = jnp.exp2(m_old - m_new)
        acc_ref[...] = alpha * acc_ref[...] + jnp.dot(va, p.astype(BF16), preferred_element_type=F32)
        m_ref[0:1] = m_new

    scores(0, sa_ref)

    def pair(t, carry):
        a = 2 * t
        scores(a + 1, sb_ref)
        consume(a, sa_ref)
        scores(a + 2, sa_ref)
        consume(a + 1, sb_ref)
        return carry

    first_diag = ratio * i
    lax.fori_loop(0, first_diag // 2, pair, 0)
    for d in range(0, ratio, 2):
        scores(first_diag + d + 1, sb_ref)
        consume(first_diag + d, sa_ref, diag_tile=d)
        if d + 2 < ratio:
            scores(first_diag + d + 2, sa_ref)
        consume(first_diag + d + 1, sb_ref, diag_tile=d + 1)

    lam = _lambda(lq1_ref, lk1_ref, lq2_ref, lk2_ref, lam_init)
    a1 = acc_ref[:, :tq]
    a2 = acc_ref[:, tq:]
    o = a1[:dv] / a1[dv:dv + 1] - lam * (a2[:dv] / a2[dv:dv + 1])
    on = o * lax.rsqrt(jnp.mean(o * o, axis=0, keepdims=True) + NORM_EPS) * subg_ref[...] * (1.0 - lam_init)
    o_ref[...] = jnp.transpose(on).astype(o_ref.dtype)


def _bf16_split(x, n_parts):
    parts = []
    rest = np.asarray(x, np.float32)
    for _ in range(n_parts):
        part = rest.astype(BF16).astype(np.float32)
        parts.append(part)
        rest = rest - part
    return parts


def _flash(qt, kb, vt, lams, sub_g, *, tq, lam_init):
    h, n_kt, dh2, tk = vt.shape
    t_len = kb.shape[1]
    dh = dh2 // 2
    tq = max(min(tq, t_len), 2 * tk)
    ratio = tq // tk
    assert tq % tk == 0 and ratio % 2 == 0 and t_len % tq == 0
    slopes_np = _alibi_slopes()
    slopes = jnp.asarray(np.broadcast_to(slopes_np[:, None, None], (h, 1, LANES)), F32)
    local = (np.arange(tk, dtype=np.float64)[None, :] * (slopes_np * LOG2E)[:, None]).astype(np.float32)
    kfeat_np = np.zeros((h, tk, LANES), np.float32)
    for col, part in enumerate(_bf16_split(local, N_BIAS_FEATURES)):
        kfeat_np[:, :, col] = part
    kfeat = jnp.asarray(kfeat_np, BF16)
    key = np.arange(tk)[None, :, None] + tk * np.arange(ratio)[:, None, None]
    masks = jnp.asarray(np.where(key <= np.arange(tq)[None, None, :], 0.0, NEG_BIG), F32)
    lam_specs = [_resident((1, dh))] * 4
    est = (2 * 2 * t_len * dh2 * 2 + ratio * tq * tk * 4 + 2 * tq * tk * 4 * 5
           + 2 * tq * 2 * dh2 * 4 * 3 + 4 * tq * dh2 * 2 * 2)
    return pl.pallas_call(
        functools.partial(_flash_kernel, tq=tq, tk=tk, dh=dh, lam_init=lam_init),
        out_shape=jax.ShapeDtypeStruct((t_len, h * dh2), BF16),
        grid=(h, t_len // tq),
        in_specs=[
            _resident(masks.shape),
            pl.BlockSpec((None, t_len, dh2), lambda hh, i: (hh, 0, 0)),
            pl.BlockSpec((None, n_kt, dh2, tk), lambda hh, i: (hh, 0, 0, 0)),
            pl.BlockSpec((None, ratio, dh2, tk), lambda hh, i: (hh, i, 0, 0)),
            pl.BlockSpec((None, tk, LANES), lambda hh, i: (hh, 0, 0)),
            _resident((dh2, 1)),
            pl.BlockSpec((None, 1, LANES), lambda hh, i: (hh, 0, 0)),
        ] + lam_specs,
        out_specs=pl.BlockSpec((tq, dh2), lambda hh, i: (i, hh)),
        scratch_shapes=[pltpu.VMEM((dh2 + LANES, 2 * tq), BF16),
                        pltpu.VMEM((tk, 2 * tq), F32), pltpu.VMEM((tk, 2 * tq), F32),
                        pltpu.VMEM((dh2 + BF16_SUBLANES, 2 * tq), F32), pltpu.VMEM((F32_SUBLANES, 2 * tq), F32)],
        compiler_params=pltpu.CompilerParams(
            dimension_semantics=("arbitrary", "arbitrary"), vmem_limit_bytes=_vmem_limit(est + (8 << 20))),
        name="flash_diff_attn",
    )(masks, kb, vt, qt, kfeat, sub_g.reshape(dh2, 1), slopes, *[a.reshape(1, dh) for a in lams])


def _dec_attn_kernel(pt_ref, *rest, n_pg, page, past_len, dh, lam_init):
    del pt_ref
    k_refs = rest[:n_pg]
    v_refs = rest[n_pg:2 * n_pg]
    (feat_ref, hs_ref, q_ref, kn_ref, vn_ref, subg_ref, slope_ref, lq1_ref, lk1_ref, lq2_ref, lk2_ref,
     o_ref, sc_ref, acc_ref, m_ref, l_ref) = rest[2 * n_pg:]
    g = pl.program_id(1)
    dh2 = 2 * dh
    nh = q_ref.shape[0]

    @pl.when(g == 0)
    def _():
        m_ref[...] = jnp.full_like(m_ref, NEG_BIG)
        l_ref[...] = jnp.zeros_like(l_ref)
        acc_ref[...] = jnp.zeros_like(acc_ref)

    q = q_ref[...]
    slope2 = slope_ref[...] * LOG2E
    feat = feat_ref[...]
    hs = hs_ref[...]

    def over_positions(x, op):
        n = x.shape[0]
        if n % POSITION_CHAINS == 0 and n > POSITION_CHAINS:
            x = op(x.reshape(POSITION_CHAINS, n // POSITION_CHAINS, nh, x.shape[-1]), axis=1)
        return op(x, axis=0)

    def update(ss, offs, vs):
        m_old = m_ref[...]
        m_new = m_old
        for s, off in zip(ss, offs):
            m_new = jnp.maximum(m_new, over_positions(s, jnp.max) + off)
        l_add = jnp.zeros_like(m_old)
        a_add = jnp.zeros_like(m_old)
        b_add = jnp.zeros_like(m_old)
        for s, off, v in zip(ss, offs, vs):
            p = jnp.exp2(s - (m_new - off)[None])
            l_add = l_add + over_positions(p, jnp.sum)
            a_add = a_add + over_positions(p * v, jnp.sum)
            b_add = b_add + over_positions(p * pltpu.roll(v, dh, 2), jnp.sum)
        alpha = jnp.exp2(m_old - m_new)
        l_ref[...] = alpha * l_ref[...] + l_add
        acc_ref[0] = alpha * acc_ref[0] + a_add
        acc_ref[1] = alpha * acc_ref[1] + b_add
        m_ref[...] = m_new

    offs = []
    for pi in range(n_pg):
        kp = k_refs[pi][...]
        prod = (kp * q[None]).astype(BF16).reshape(page * nh, dh2)
        sc_ref[pi] = jnp.dot(jnp.concatenate([prod, feat], axis=1), hs, preferred_element_type=F32)
        first_pos = (g * n_pg + pi) * page - past_len
        offs.append(jnp.full((1, 1), first_pos, jnp.int32).astype(F32) * slope2)
    for g0 in range(0, n_pg, DECODE_PAGES_PER_UPDATE):
        pages = range(g0, g0 + DECODE_PAGES_PER_UPDATE)
        update([sc_ref[pi].reshape(page, nh, LANES) for pi in pages], [offs[pi] for pi in pages],
               [v_refs[pi][...] for pi in pages])

    @pl.when(g == pl.num_programs(1) - 1)
    def _():
        kn = kn_ref[...]
        vn = vn_ref[...]
        s_new = jnp.dot((kn * q).astype(BF16), hs[:dh2], preferred_element_type=F32)
        update([s_new[None]], [jnp.zeros_like(slope2)], [vn[None]])
        lam = _lambda(lq1_ref, lk1_ref, lq2_ref, lk2_ref, lam_init)
        first_half = lax.broadcasted_iota(jnp.int32, (nh, LANES), 1) < dh

        def swap(x):
            return pltpu.roll(x, dh, 1)

        l = l_ref[...]
        l1 = jnp.where(first_half, l, swap(l))
        l2 = jnp.where(first_half, swap(l), l)
        a = acc_ref[0]
        b = acc_ref[1]
        o1 = jnp.where(first_half, a, swap(b))
        o2 = jnp.where(first_half, swap(b), a)
        o = o1 / l1 - lam * (o2 / l2)
        o_ref[...] = _sub_norm(o, subg_ref[...], lam_init)


def _dec_attn(q, kn, vn, cache_k, cache_v, layer, page_table, lams, sub_g, *, lam_init, n_pg):
    b, nh, dh2 = q.shape
    dh = dh2 // 2
    n_pages = page_table.shape[1]
    page = cache_k.shape[2]
    while n_pages % n_pg:
        n_pg //= 2
    slopes = jnp.asarray(np.broadcast_to(_alibi_slopes()[:, None], (nh, LANES)), F32)
    local = (np.arange(page, dtype=np.float64)[:, None] * (_alibi_slopes() * LOG2E)[None, :]).astype(np.float32)
    feat_np = np.zeros((page * nh, LANES), np.float32)
    for col, part in enumerate(_bf16_split(local, N_BIAS_FEATURES)):
        feat_np[:, col] = part.reshape(-1)
    feat = jnp.asarray(feat_np, BF16)
    hs_np = np.zeros((dh2 + LANES, LANES), np.float32)
    hs_np[:dh, :dh] = 1.0
    hs_np[dh:dh2, dh:] = 1.0
    hs_np[dh2:dh2 + N_BIAS_FEATURES, :] = 1.0
    hs = jnp.asarray(hs_np, BF16)

    def page_spec(pi):
        return pl.BlockSpec((None, None, page, nh, dh2),
                            lambda bb, g, pt: (layer, pt[bb, g * n_pg + pi], 0, 0, 0))

    tok = pl.BlockSpec((None, nh, dh2), lambda bb, g, pt: (bb, 0, 0))

    def const(shape):
        zeros = (0,) * len(shape)
        return pl.BlockSpec(shape, lambda bb, g, pt: zeros)

    grid_spec = pltpu.PrefetchScalarGridSpec(
        num_scalar_prefetch=1,
        grid=(b, n_pages // n_pg),
        in_specs=[page_spec(pi) for pi in range(n_pg)] * 2
                 + [const(feat.shape), const(hs.shape), tok, tok, tok, const((1, dh2)), const((nh, LANES))]
                 + [const((1, dh))] * 4,
        out_specs=tok,
        scratch_shapes=[pltpu.VMEM((n_pg, page * nh, LANES), F32), pltpu.VMEM((2, nh, dh2), F32),
                        pltpu.VMEM((nh, LANES), F32), pltpu.VMEM((nh, LANES), F32)],
    )
    est = 2 * 2 * n_pg * page * nh * dh2 * 4 + n_pg * page * nh * LANES * 4 + page * nh * LANES * 4 * 12
    return pl.pallas_call(
        functools.partial(_dec_attn_kernel, n_pg=n_pg, page=page, past_len=n_pages * page, dh=dh,
                          lam_init=lam_init),
        out_shape=jax.ShapeDtypeStruct((b, nh, dh2), F32),
        grid_spec=grid_spec,
        compiler_params=pltpu.CompilerParams(
            dimension_semantics=("arbitrary", "arbitrary"), vmem_limit_bytes=_vmem_limit(est + (8 << 20))),
        name="decode_diff_attn",
    )(page_table, *([cache_k] * n_pg), *([cache_v] * n_pg), feat, hs, q, kn, vn, sub_g.reshape(1, dh2), slopes,
      *[a.reshape(1, dh) for a in lams])


FFN_CHUNK = MXU_TILE
ROW_TILE = 512
FLASH_Q_TILE = 1024
FLASH_K_TILE = 512
DECODE_PAGES_PER_STEP = 8
DECODE_PAGES_PER_UPDATE = 2
POSITION_CHAINS = 8


def _prep_ffn_weights(w_gu, w_d):
    assert w_d.shape[0] % FFN_CHUNK == 0
    return w_gu.astype(BF16), w_d.astype(BF16)


def kernel(x_prompt, x_sample, state_ret, cache_k, cache_v, page_table, norm_g, ffn_w_gu, ffn_w_d, final_g,
           ret_w_in, ret_w_out, ret_gn_g, ret_gn_b, att_w_in, att_w_out, att_lam_q1, att_lam_k1, att_lam_q2,
           att_lam_k2, att_subln_g):
    depth = norm_g.shape[0]
    batch, seq, d_model = x_prompt.shape
    dec_batch, dec_seq, _ = x_sample.shape
    assert batch == 1 and dec_seq == 1
    ret_dk = d_model // RET_HEADS
    ret_dv = 2 * d_model // RET_HEADS
    att_dh2 = d_model // ATT_HEADS

    yp = x_prompt.reshape(seq, d_model)
    ys = x_sample.reshape(dec_batch, d_model)
    ret_p, ret_s, kp_rows, vp_rows, ks_rows, vs_rows = [], [], [], [], [], []
    for i in range(depth):
        r = i // N_MIXERS
        last = i == depth - 1
        wgu0, wd0 = _prep_ffn_weights(ffn_w_gu[i, 0], ffn_w_d[i, 0])
        wgu1, wd1 = _prep_ffn_weights(ffn_w_gu[i, 1], ffn_w_d[i, 1])
        yp = _ffn(yp, norm_g[i, 0], wgu0, wd0, final_g, tm=ROW_TILE, final=False)
        ys = _ffn(ys, norm_g[i, 0], wgu0, wd0, final_g, tm=ROW_TILE, final=False)
        if i % N_MIXERS == 0:
            w_in = ret_w_in[r].astype(BF16)
            w_out = ret_w_out[r].astype(BF16)
            k_scale = ret_dk ** -0.5
            qkvg_p = _ret_proj(yp, norm_g[i, 1], w_in, tm=ROW_TILE, out_dtype=BF16, k_scale=k_scale)
            qkvg_s = _ret_proj(ys, norm_g[i, 1], w_in, tm=ROW_TILE, out_dtype=F32, k_scale=k_scale)
            op, sp = _ret_scan(qkvg_p, ret_gn_g[r], ret_gn_b[r], dk=ret_dk, dv=ret_dv)
            osm, ss = _ret_step(qkvg_s, state_ret, r, ret_gn_g[r], ret_gn_b[r], dk=ret_dk, dv=ret_dv)
            ret_p.append(sp.reshape(1, RET_HEADS, ret_dk, ret_dv))
            ret_s.append(ss)
        else:
            w_in = att_w_in[r].astype(BF16)
            w_out = att_w_out[r].astype(BF16)
            lam_init = 0.8 - 0.6 * math.exp(-0.3 * i)
            lams = (att_lam_q1[r], att_lam_k1[r], att_lam_q2[r], att_lam_k2[r])
            kp, vp, qt, kb, vt = _att_proj(yp, norm_g[i, 1], w_in, tm=FLASH_K_TILE, head_major=True)
            qs, ksm, vsm = _att_proj(ys, norm_g[i, 1], w_in, tm=ROW_TILE, head_major=False)
            op = _flash(qt, kb, vt, lams, att_subln_g[r], tq=FLASH_Q_TILE, lam_init=lam_init)
            hs = (dec_batch, ATT_HEADS, att_dh2)
            osm = _dec_attn(qs.reshape(hs), ksm.reshape(hs), vsm.reshape(hs), cache_k, cache_v, r, page_table,
                            lams, att_subln_g[r], lam_init=lam_init, n_pg=DECODE_PAGES_PER_STEP)
            osm = osm.reshape(dec_batch, d_model)
            kp_rows.append(kp.reshape(1, seq, ATT_HEADS, att_dh2))
            vp_rows.append(vp.reshape(1, seq, ATT_HEADS, att_dh2))
            ks_rows.append(ksm.reshape(dec_batch, 1, ATT_HEADS, att_dh2))
            vs_rows.append(vsm.reshape(dec_batch, 1, ATT_HEADS, att_dh2))
        yp = _ffn(yp, norm_g[i, 2], wgu1, wd1, final_g, tm=ROW_TILE, final=last, mix=op, w_out=w_out)
        ys = _ffn(ys, norm_g[i, 2], wgu1, wd1, final_g, tm=ROW_TILE, final=last, mix=osm, w_out=w_out)
    return (yp.reshape(1, seq, d_model), ys.reshape(dec_batch, 1, d_model),
            jnp.stack(ret_p), jnp.stack(kp_rows), jnp.stack(vp_rows),
            jnp.stack(ret_s), jnp.stack(ks_rows), jnp.stack(vs_rows))
```

```python
import functools
import math

import numpy as np
import jax
import jax.numpy as jnp
from jax import lax
from jax.experimental import pallas as pl
from jax.experimental.pallas import tpu as pltpu

F32 = jnp.float32
BF16 = jnp.bfloat16

NORM_EPS = 1e-6
GN_EPS = 1e-5
NEG_BIG = -1e30
LOG2E = 1.4426950408889634
N_MIXERS = 2

RET_HEADS = 8
RET_CHUNK = 128
RET_CHUNKS_PER_STEP = 4
RET_SEQS_PER_STEP = 4
ATT_HEADS = 8

V7X_VMEM_BYTES = 64 * 1024 * 1024
MXU_TILE = 256
LANES = 128
BF16_SUBLANES = 16
F32_SUBLANES = 8
N_BIAS_FEATURES = 3


def _vmem_limit(nbytes):
    return int(min(V7X_VMEM_BYTES - 8 * 1024 * 1024, max(32 * 1024 * 1024, nbytes)))


def _rms(x, g):
    return x * lax.rsqrt(jnp.mean(x * x, axis=-1, keepdims=True) + NORM_EPS) * g


def _resident(shape):
    zeros = (0,) * len(shape)
    return pl.BlockSpec(shape, lambda *_: zeros, pipeline_mode=pl.Buffered(1))


def _ffn_kernel(x_ref, *refs, d_ff, ck, final, mixed):
    if mixed:
        mix_ref, wo_ref, wgu_ref, wd_ref, g_ref, fg_ref, o_ref, xn_ref, acc_ref = refs
        x = x_ref[...] + jnp.dot(mix_ref[...].astype(BF16), wo_ref[...], preferred_element_type=F32)
    else:
        wgu_ref, wd_ref, g_ref, fg_ref, o_ref, xn_ref, acc_ref = refs
        x = x_ref[...]
    xn_ref[...] = _rms(x, g_ref[...]).astype(BF16)
    acc_ref[...] = jnp.zeros_like(acc_ref)
    for c0 in range(0, d_ff, ck):
        xn = xn_ref[...]
        gate = jnp.dot(xn, wgu_ref[:, c0:c0 + ck], preferred_element_type=F32)
        up = jnp.dot(xn, wgu_ref[:, d_ff + c0:d_ff + c0 + ck], preferred_element_type=F32)
        h = (jax.nn.silu(gate) * up).astype(BF16)
        acc_ref[...] += jnp.dot(h, wd_ref[c0:c0 + ck, :], preferred_element_type=F32)
    y = x + 0.5 * acc_ref[...]
    if final:
        y = _rms(y, fg_ref[...])
    o_ref[...] = y


def _ffn(x, g, wgu_r, wd_r, final_g, *, tm, final, mix=None, w_out=None):
    m, d = x.shape
    d_ff = wd_r.shape[0]
    ck = FFN_CHUNK
    tm = min(tm, m)
    mixed = mix is not None
    row = pl.BlockSpec((tm, d), lambda i: (i, 0))
    est = (wgu_r.size + wd_r.size) * 2 + tm * d * (4 * 4 + 2 + 4) + tm * 2 * ck * 4 * 3
    mix_specs, mix_args = [], []
    if mixed:
        kdim = mix.shape[1]
        mix_specs = [pl.BlockSpec((tm, kdim), lambda i: (i, 0)), _resident(w_out.shape)]
        mix_args = [mix, w_out]
        est += w_out.size * 2 + tm * kdim * 4 * 2 + tm * d * 4
    return pl.pallas_call(
        functools.partial(_ffn_kernel, d_ff=d_ff, ck=ck, final=final, mixed=mixed),
        out_shape=jax.ShapeDtypeStruct((m, d), F32),
        grid=(m // tm,),
        in_specs=[row] + mix_specs + [_resident(wgu_r.shape), _resident(wd_r.shape),
                                      _resident((1, d)), _resident((1, d))],
        out_specs=row,
        scratch_shapes=[pltpu.VMEM((tm, d), BF16), pltpu.VMEM((tm, d), F32)],
        compiler_params=pltpu.CompilerParams(
            dimension_semantics=("arbitrary",), vmem_limit_bytes=_vmem_limit(est + (8 << 20))),
        name="ffn",
    )(x, *mix_args, wgu_r, wd_r, g.reshape(1, d), final_g.reshape(1, d))


def _ret_proj_kernel(x_ref, g_ref, w_ref, o_ref, *, d, k_scale):
    xn = _rms(x_ref[...], g_ref[...]).astype(BF16)
    n = w_ref.shape[1]
    for c in range(n // d):
        r = jnp.dot(xn, w_ref[:, c * d:(c + 1) * d], preferred_element_type=F32)
        if c == 1:
            r = r * k_scale
        o_ref[:, c * d:(c + 1) * d] = r.astype(o_ref.dtype)


def _ret_proj(x, g, w, *, tm, out_dtype, k_scale):
    m, d = x.shape
    n = w.shape[1]
    tm = min(tm, m)
    est = w.size * 2 + tm * d * 4 * 2 + tm * n * 4 * 2 + tm * d * 4 * 3
    return pl.pallas_call(
        functools.partial(_ret_proj_kernel, d=d, k_scale=k_scale),
        out_shape=jax.ShapeDtypeStruct((m, n), out_dtype),
        grid=(m // tm,),
        in_specs=[pl.BlockSpec((tm, d), lambda i: (i, 0)), _resident((1, d)), _resident(w.shape)],
        out_specs=pl.BlockSpec((tm, n), lambda i: (i, 0)),
        compiler_params=pltpu.CompilerParams(
            dimension_semantics=("arbitrary",), vmem_limit_bytes=_vmem_limit(est + (8 << 20))),
        name="ret_proj",
    )(x, g.reshape(1, d), w)


def _ret_consts(chunk):
    lg = np.log1p(-np.exp2(-5.0 - np.arange(RET_HEADS, dtype=np.float64)))
    pos = np.arange(chunk, dtype=np.float64)
    n = pos[:, None] - pos[None, :]
    decay = np.where(n[None] >= 0, np.exp(lg[:, None, None] * np.maximum(n, 0.0)[None]), 0.0)
    q_dec = np.exp(lg[:, None] * (pos[None, :] + 1.0))
    k_dec = np.exp(lg[:, None] * (chunk - 1.0 - pos)[None, :])
    s_dec = np.exp(lg * chunk)
    return decay, q_dec, k_dec, s_dec


def _group_norm_gate(o, gate, gn_g, gn_b):
    mu = jnp.mean(o, axis=-1, keepdims=True)
    var = jnp.mean(jnp.square(o - mu), axis=-1, keepdims=True)
    on = (o - mu) * lax.rsqrt(var + GN_EPS)
    on = on * gn_g + gn_b
    return on * jax.nn.silu(gate)


def _ret_scan_kernel(q_ref, k_ref, v_ref, gt_ref, dec_ref, qd_ref, kd_ref, gng_ref, gnb_ref,
                     o_ref, s_out_ref, s_ref, *, dk, dv, s_dec):
    t = pl.program_id(0)

    @pl.when(t == 0)
    def _():
        s_ref[...] = jnp.zeros_like(s_ref)

    c = dec_ref.shape[1]
    for r0 in range(0, q_ref.shape[0], c):
        rows = slice(r0, r0 + c)
        for h in range(RET_HEADS):
            qh = q_ref[rows, h * dk:(h + 1) * dk]
            kh = k_ref[rows, h * dk:(h + 1) * dk]
            vh = v_ref[rows, h * dv:(h + 1) * dv]
            sc = lax.dot_general(qh, kh, (((1,), (1,)), ((), ())), preferred_element_type=F32) * dec_ref[h]
            o = jnp.dot(sc.astype(BF16), vh, preferred_element_type=F32)
            s_old = s_ref[h]
            qd = (qh.astype(F32) * qd_ref[h]).astype(BF16)
            o = o + jnp.dot(qd, s_old.astype(BF16), preferred_element_type=F32)
            kd = (kh.astype(F32) * kd_ref[h]).astype(BF16)
            s_ref[h] = s_dec[h] * s_old + lax.dot_general(kd, vh, (((0,), (0,)), ((), ())),
                                                          preferred_element_type=F32)
            gate = gt_ref[rows, h * dv:(h + 1) * dv].astype(F32)
            res = _group_norm_gate(o, gate, gng_ref[:, h * dv:(h + 1) * dv], gnb_ref[:, h * dv:(h + 1) * dv])
            o_ref[rows, h * dv:(h + 1) * dv] = res.astype(o_ref.dtype)

    @pl.when(t == pl.num_programs(0) - 1)
    def _():
        s_out_ref[...] = s_ref[...]


def _ret_scan(qkvg, gn_g, gn_b, *, dk, dv):
    t_len = qkvg.shape[0]
    c = RET_CHUNK if t_len % RET_CHUNK == 0 else t_len
    hk, hv = RET_HEADS * dk, RET_HEADS * dv
    decay, q_dec, k_dec, s_dec = _ret_consts(c)
    dec = jnp.asarray(decay, F32)
    qd = jnp.asarray(np.broadcast_to(q_dec[:, :, None], (RET_HEADS, c, dk)), F32)
    kd = jnp.asarray(np.broadcast_to(k_dec[:, :, None], (RET_HEADS, c, dk)), F32)
    rows = c * RET_CHUNKS_PER_STEP if t_len % (c * RET_CHUNKS_PER_STEP) == 0 else c
    return pl.pallas_call(
        functools.partial(_ret_scan_kernel, dk=dk, dv=dv, s_dec=tuple(float(s) for s in s_dec)),
        out_shape=(jax.ShapeDtypeStruct((t_len, hv), BF16),
                   jax.ShapeDtypeStruct((RET_HEADS, dk, dv), F32)),
        grid=(t_len // rows,),
        in_specs=[
            pl.BlockSpec((rows, hk), lambda t: (t, 0)),
            pl.BlockSpec((rows, hk), lambda t: (t, 1)),
            pl.BlockSpec((rows, hv), lambda t: (t, 1)),
            pl.BlockSpec((rows, hv), lambda t: (t, 2)),
            _resident(dec.shape), _resident(qd.shape), _resident(kd.shape),
            _resident((1, hv)), _resident((1, hv)),
        ],
        out_specs=(pl.BlockSpec((rows, hv), lambda t: (t, 0)),
                   pl.BlockSpec((RET_HEADS, dk, dv), lambda t: (0, 0, 0))),
        scratch_shapes=[pltpu.VMEM((RET_HEADS, dk, dv), F32)],
        compiler_params=pltpu.CompilerParams(dimension_semantics=("arbitrary",)),
        name="ret_scan",
    )(qkvg, qkvg, qkvg, qkvg, dec, qd, kd, gn_g.reshape(1, hv), gn_b.reshape(1, hv))


def _ret_step_kernel(x_ref, s_ref, gng_ref, gnb_ref, o_ref, s_out_ref, *, dk, dv, gamma):
    nh = RET_HEADS
    per_head = dv // LANES

    def head_row(x, base, h):
        r0 = base + per_head * h
        return jnp.concatenate([x[r0 + j:r0 + j + 1] for j in range(per_head)], axis=1)

    for b in range(x_ref.shape[0]):
        x = x_ref[b]
        qk_cols = jnp.transpose(jnp.concatenate([x[:2 * nh], jnp.zeros((dk - 2 * nh, dk), F32)], axis=0))
        for h in range(nh):
            q = x[h:h + 1]
            k = x[nh + h:nh + h + 1]
            v = head_row(x, 2 * nh, h)
            gate = head_row(x, 2 * nh + per_head * nh, h)
            q_col = qk_cols[:, h:h + 1]
            k_col = qk_cols[:, nh + h:nh + h + 1]
            s_old = s_ref[b, h]
            qk = jnp.sum(q * k, axis=1, keepdims=True)
            o = qk * v + jnp.sum((q_col * gamma[h]) * s_old, axis=0, keepdims=True)
            s_out_ref[b, h] = gamma[h] * s_old + k_col * v
            res = _group_norm_gate(o, gate, gng_ref[:, h * dv:(h + 1) * dv], gnb_ref[:, h * dv:(h + 1) * dv])
            o_ref[b, :, h * dv:(h + 1) * dv] = res


def _ret_step(qkvg, state, layer, gn_g, gn_b, *, dk, dv):
    b, n = qkvg.shape
    hv = RET_HEADS * dv
    assert dk == LANES and dv % LANES == 0
    nb = RET_SEQS_PER_STEP if b % RET_SEQS_PER_STEP == 0 else 1
    gamma = tuple(float(1.0 - 2.0 ** (-5.0 - h)) for h in range(RET_HEADS))
    o, s_new = pl.pallas_call(
        functools.partial(_ret_step_kernel, dk=dk, dv=dv, gamma=gamma),
        out_shape=(jax.ShapeDtypeStruct((b, 1, hv), F32),
                   jax.ShapeDtypeStruct(state.shape[1:], F32)),
        grid=(b // nb,),
        in_specs=[
            pl.BlockSpec((nb, n // LANES, LANES), lambda i: (i, 0, 0)),
            pl.BlockSpec((None, nb, RET_HEADS, dk, dv), lambda i: (layer, i, 0, 0, 0)),
            _resident((1, hv)), _resident((1, hv)),
        ],
        out_specs=(pl.BlockSpec((nb, 1, hv), lambda i: (i, 0, 0)),
                   pl.BlockSpec((nb, RET_HEADS, dk, dv), lambda i: (i, 0, 0, 0))),
        compiler_params=pltpu.CompilerParams(dimension_semantics=("arbitrary",)),
        name="ret_step",
    )(qkvg.reshape(b, n // LANES, LANES), state, gn_g.reshape(1, hv), gn_b.reshape(1, hv))
    return o.reshape(b, hv), s_new


def _att_proj_kernel(x_ref, g_ref, w_ref, *out_refs, d, dh2, q_scale, head_major):
    xn = _rms(x_ref[...], g_ref[...]).astype(BF16)
    q = jnp.dot(xn, w_ref[:, 0:d], preferred_element_type=F32) * q_scale
    k = jnp.dot(xn, w_ref[:, d:2 * d], preferred_element_type=F32)
    v = jnp.dot(xn, w_ref[:, 2 * d:3 * d], preferred_element_type=F32)
    if head_major:
        k_ref, v_ref, qt_ref, kb_ref, vt_ref = out_refs
        k_ref[...] = k
        v_ref[...] = v
        for h in range(ATT_HEADS):
            qt_ref[h, 0] = jnp.transpose(q[:, h * dh2:(h + 1) * dh2]).astype(BF16)
            kb_ref[h] = k[:, h * dh2:(h + 1) * dh2].astype(BF16)
            vt_ref[h, 0] = jnp.transpose(v[:, h * dh2:(h + 1) * dh2]).astype(BF16)
    else:
        q_ref, k_ref, v_ref = out_refs
        q_ref[...] = q
        k_ref[...] = k
        v_ref[...] = v


def _att_proj(x, g, w, *, tm, head_major):
    m, d = x.shape
    dh2 = d // ATT_HEADS
    tm = min(tm, m)
    row = pl.BlockSpec((tm, d), lambda i: (i, 0))
    if head_major:
        hm = pl.BlockSpec((ATT_HEADS, tm, dh2), lambda i: (0, i, 0))
        hm_t = pl.BlockSpec((ATT_HEADS, 1, dh2, tm), lambda i: (0, i, 0, 0))
        t_shape = jax.ShapeDtypeStruct((ATT_HEADS, m // tm, dh2, tm), BF16)
        out_shape = (jax.ShapeDtypeStruct((m, d), F32),) * 2 + (
            t_shape, jax.ShapeDtypeStruct((ATT_HEADS, m, dh2), BF16), t_shape)
        out_specs = (row, row, hm_t, hm, hm_t)
    else:
        out_shape = (jax.ShapeDtypeStruct((m, d), F32),) * 3
        out_specs = (row, row, row)
    est = w.size * 2 + tm * d * 4 * 12
    return pl.pallas_call(
        functools.partial(_att_proj_kernel, d=d, dh2=dh2, q_scale=(dh2 // 2) ** -0.5 * LOG2E,
                          head_major=head_major),
        out_shape=out_shape,
        grid=(m // tm,),
        in_specs=[row, _resident((1, d)), _resident(w.shape)],
        out_specs=out_specs,
        compiler_params=pltpu.CompilerParams(
            dimension_semantics=("arbitrary",), vmem_limit_bytes=_vmem_limit(est + (8 << 20))),
        name="att_proj",
    )(x, g.reshape(1, d), w)


def _alibi_slopes():
    return np.exp2(-8.0 * (np.arange(ATT_HEADS, dtype=np.float64) + 1.0) / ATT_HEADS)


def _lambda(lq1_ref, lk1_ref, lq2_ref, lk2_ref, lam_init):
    a = jnp.sum(lq1_ref[...] * lk1_ref[...], axis=1, keepdims=True)
    b = jnp.sum(lq2_ref[...] * lk2_ref[...], axis=1, keepdims=True)
    return jnp.exp(a) - jnp.exp(b) + lam_init


def _sub_norm(o, sub_g, lam_init):
    return _rms(o, sub_g) * (1.0 - lam_init)


def _flash_kernel(mask_ref, k_ref, vt_ref, qt_ref, kfeat_ref, subg_ref, slope_ref, lq1_ref, lk1_ref, lq2_ref, lk2_ref,
                  o_ref, qz_ref, sa_ref, sb_ref, acc_ref, m_ref, *, tq, tk, dh, lam_init):
    i = pl.program_id(1)
    dv = 2 * dh
    dh2 = 2 * dh
    ratio = tq // tk
    slope2 = slope_ref[:, :1] * LOG2E

    qt = jnp.concatenate([qt_ref[b] for b in range(qt_ref.shape[0])], axis=1)
    feat = lax.broadcasted_iota(jnp.int32, qt.shape, 0)
    zero = jnp.zeros_like(qt)
    qz_ref[:dh2, :tq] = jnp.where(feat < dh, qt, zero)
    qz_ref[:dh2, tq:] = jnp.where(feat >= dh, qt, zero)
    bias_row = lax.broadcasted_iota(jnp.int32, (LANES, 2 * tq), 0) < N_BIAS_FEATURES
    qz_ref[dh2:, :] = jnp.where(bias_row, 1.0, 0.0).astype(BF16)
    m_ref[...] = jnp.full_like(m_ref, NEG_BIG)
    acc_ref[...] = jnp.zeros_like(acc_ref)
    ones = jnp.ones((BF16_SUBLANES, tk), BF16)
    kfeat = kfeat_ref[...]

    def scores(j, s_ref):
        kt = k_ref[pl.ds(pl.multiple_of(j * tk, tk), tk), :]
        s_ref[...] = jnp.dot(jnp.concatenate([kt, kfeat], axis=1), qz_ref[...],
                             preferred_element_type=F32)

    def consume(j, s_ref, diag_tile=None):
        va = jnp.concatenate([vt_ref[j], ones], axis=0)
        blk = jnp.full((1, 1), j * tk - i * tq, jnp.int32).astype(F32) * slope2
        s = s_ref[...]
        if diag_tile is not None:
            mask = mask_ref[diag_tile]
            s = s + jnp.concatenate([mask, mask], axis=1)
        m_old = m_ref[0:1]
        m_new = jnp.maximum(m_old, jnp.max(s, axis=0, keepdims=True) + blk)
        p = jnp.exp2(s - (m_new - blk))
        alpha = jnp.exp2(m_old - m_new)
        acc_ref[...] = alpha * acc_ref[...] + jnp.dot(va, p.astype(BF16), preferred_element_type=F32)
        m_ref[0:1] = m_new

    scores(0, sa_ref)

    def pair(t):
        a = 2 * t
        scores(a + 1, sb_ref)
        consume(a, sa_ref)
        scores(a + 2, sa_ref)
        consume(a + 1, sb_ref)

    def two_pairs(t, carry):
        pair(2 * t)
        pair(2 * t + 1)
        return carry

    first_diag = ratio * i
    n_pairs = first_diag // 2
    lax.fori_loop(0, n_pairs // 2, two_pairs, 0)

    @pl.when(n_pairs % 2 == 1)
    def _():
        pair(n_pairs - 1)

    for d in range(0, ratio, 2):
        scores(first_diag + d + 1, sb_ref)
        consume(first_diag + d, sa_ref, diag_tile=d)
        if d + 2 < ratio:
            scores(first_diag + d + 2, sa_ref)
        consume(first_diag + d + 1, sb_ref, diag_tile=d + 1)

    lam = _lambda(lq1_ref, lk1_ref, lq2_ref, lk2_ref, lam_init)
    a1 = acc_ref[:, :tq]
    a2 = acc_ref[:, tq:]
    o = a1[:dv] / a1[dv:dv + 1] - lam * (a2[:dv] / a2[dv:dv + 1])
    on = o * lax.rsqrt(jnp.mean(o * o, axis=0, keepdims=True) + NORM_EPS) * subg_ref[...] * (1.0 - lam_init)
    o_ref[...] = jnp.transpose(on).astype(o_ref.dtype)


def _bf16_split(x, n_parts):
    parts = []
    rest = np.asarray(x, np.float32)
    for _ in range(n_parts):
        part = rest.astype(BF16).astype(np.float32)
        parts.append(part)
        rest = rest - part
    return parts


def _flash(qt, kb, vt, lams, sub_g, *, tq, lam_init):
    h, n_kt, dh2, tk = vt.shape
    t_len = kb.shape[1]
    dh = dh2 // 2
    tq = max(min(tq, t_len), 2 * tk)
    ratio = tq // tk
    assert tq % tk == 0 and ratio % 2 == 0 and t_len % tq == 0
    slopes_np = _alibi_slopes()
    slopes = jnp.asarray(np.broadcast_to(slopes_np[:, None, None], (h, 1, LANES)), F32)
    local = (np.arange(tk, dtype=np.float64)[None, :] * (slopes_np * LOG2E)[:, None]).astype(np.float32)
    kfeat_np = np.zeros((h, tk, LANES), np.float32)
    for col, part in enumerate(_bf16_split(local, N_BIAS_FEATURES)):
        kfeat_np[:, :, col] = part
    kfeat = jnp.asarray(kfeat_np, BF16)
    key = np.arange(tk)[None, :, None] + tk * np.arange(ratio)[:, None, None]
    masks = jnp.asarray(np.where(key <= np.arange(tq)[None, None, :], 0.0, NEG_BIG), F32)
    lam_specs = [_resident((1, dh))] * 4
    est = (2 * 2 * t_len * dh2 * 2 + ratio * tq * tk * 4 + 2 * tq * tk * 4 * 5
           + 2 * tq * 2 * dh2 * 4 * 3 + 4 * tq * dh2 * 2 * 2)
    return pl.pallas_call(
        functools.partial(_flash_kernel, tq=tq, tk=tk, dh=dh, lam_init=lam_init),
        out_shape=jax.ShapeDtypeStruct((t_len, h * dh2), BF16),
        grid=(h, t_len // tq),
        in_specs=[
            _resident(masks.shape),
            pl.BlockSpec((None, t_len, dh2), lambda hh, i: (hh, 0, 0)),
            pl.BlockSpec((None, n_kt, dh2, tk), lambda hh, i: (hh, 0, 0, 0)),
            pl.BlockSpec((None, ratio, dh2, tk), lambda hh, i: (hh, i, 0, 0)),
            pl.BlockSpec((None, tk, LANES), lambda hh, i: (hh, 0, 0)),
            _resident((dh2, 1)),
            pl.BlockSpec((None, 1, LANES), lambda hh, i: (hh, 0, 0)),
        ] + lam_specs,
        out_specs=pl.BlockSpec((tq, dh2), lambda hh, i: (i, hh)),
        scratch_shapes=[pltpu.VMEM((dh2 + LANES, 2 * tq), BF16),
                        pltpu.VMEM((tk, 2 * tq), F32), pltpu.VMEM((tk, 2 * tq), F32),
                        pltpu.VMEM((dh2 + BF16_SUBLANES, 2 * tq), F32), pltpu.VMEM((F32_SUBLANES, 2 * tq), F32)],
        compiler_params=pltpu.CompilerParams(
            dimension_semantics=("arbitrary", "arbitrary"), vmem_limit_bytes=_vmem_limit(est + (8 << 20))),
        name="flash_diff_attn",
    )(masks, kb, vt, qt, kfeat, sub_g.reshape(dh2, 1), slopes, *[a.reshape(1, dh) for a in lams])


def _dec_attn_kernel(pt_ref, *rest, n_pg, page, past_len, dh, lam_init):
    del pt_ref
    k_refs = rest[:n_pg]
    v_refs = rest[n_pg:2 * n_pg]
    (feat_ref, hs_ref, q_ref, kn_ref, vn_ref, subg_ref, slope_ref, lq1_ref, lk1_ref, lq2_ref, lk2_ref,
     o_ref, sc_ref, acc_ref, m_ref, l_ref) = rest[2 * n_pg:]
    g = pl.program_id(1)
    dh2 = 2 * dh
    nh = q_ref.shape[0]

    @pl.when(g == 0)
    def _():
        m_ref[...] = jnp.full_like(m_ref, NEG_BIG)
        l_ref[...] = jnp.zeros_like(l_ref)
        acc_ref[...] = jnp.zeros_like(acc_ref)

    q = q_ref[...]
    slope2 = slope_ref[...] * LOG2E
    feat = feat_ref[...]
    hs = hs_ref[...]

    def over_positions(x, op):
        n = x.shape[0]
        if n % POSITION_CHAINS == 0 and n > POSITION_CHAINS:
            x = op(x.reshape(POSITION_CHAINS, n // POSITION_CHAINS, nh, x.shape[-1]), axis=1)
        return op(x, axis=0)

    def update(ss, offs, vs):
        m_old = m_ref[...]
        m_new = m_old
        for s, off in zip(ss, offs):
            m_new = jnp.maximum(m_new, over_positions(s, jnp.max) + off)
        l_add = jnp.zeros_like(m_old)
        a_add = jnp.zeros_like(m_old)
        b_add = jnp.zeros_like(m_old)
        for s, off, v in zip(ss, offs, vs):
            p = jnp.exp2(s - (m_new - off)[None])
            l_add = l_add + over_positions(p, jnp.sum)
            a_add = a_add + over_positions(p * v, jnp.sum)
            b_add = b_add + over_positions(p * pltpu.roll(v, dh, 2), jnp.sum)
        alpha = jnp.exp2(m_old - m_new)
        l_ref[...] = alpha * l_ref[...] + l_add
        acc_ref[0] = alpha * acc_ref[0] + a_add
        acc_ref[1] = alpha * acc_ref[1] + b_add
        m_ref[...] = m_new

    offs = []
    for pi in range(n_pg):
        kp = k_refs[pi][...]
        prod = (kp * q[None]).astype(BF16).reshape(page * nh, dh2)
        sc_ref[pi] = jnp.dot(jnp.concatenate([prod, feat], axis=1), hs, preferred_element_type=F32)
        first_pos = (g * n_pg + pi) * page - past_len
        offs.append(jnp.full((1, 1), first_pos, jnp.int32).astype(F32) * slope2)
    for g0 in range(0, n_pg, DECODE_PAGES_PER_UPDATE):
        pages = range(g0, g0 + DECODE_PAGES_PER_UPDATE)
        update([sc_ref[pi].reshape(page, nh, LANES) for pi in pages], [offs[pi] for pi in pages],
               [v_refs[pi][...] for pi in pages])

    @pl.when(g == pl.num_programs(1) - 1)
    def _():
        kn = kn_ref[...]
        vn = vn_ref[...]
        s_new = jnp.dot((kn * q).astype(BF16), hs[:dh2], preferred_element_type=F32)
        update([s_new[None]], [jnp.zeros_like(slope2)], [vn[None]])
        lam = _lambda(lq1_ref, lk1_ref, lq2_ref, lk2_ref, lam_init)
        first_half = lax.broadcasted_iota(jnp.int32, (nh, LANES), 1) < dh

        def swap(x):
            return pltpu.roll(x, dh, 1)

        l = l_ref[...]
        l1 = jnp.where(first_half, l, swap(l))
        l2 = jnp.where(first_half, swap(l), l)
        a = acc_ref[0]
        b = acc_ref[1]
        o1 = jnp.where(first_half, a, swap(b))
        o2 = jnp.where(first_half, swap(b), a)
        o = o1 / l1 - lam * (o2 / l2)
        o_ref[...] = _sub_norm(o, subg_ref[...], lam_init)


def _dec_attn(q, kn, vn, cache_k, cache_v, layer, page_table, lams, sub_g, *, lam_init, n_pg):
    b, nh, dh2 = q.shape
    dh = dh2 // 2
    n_pages = page_table.shape[1]
    page = cache_k.shape[2]
    while n_pages % n_pg:
        n_pg //= 2
    slopes = jnp.asarray(np.broadcast_to(_alibi_slopes()[:, None], (nh, LANES)), F32)
    local = (np.arange(page, dtype=np.float64)[:, None] * (_alibi_slopes() * LOG2E)[None, :]).astype(np.float32)
    feat_np = np.zeros((page * nh, LANES), np.float32)
    for col, part in enumerate(_bf16_split(local, N_BIAS_FEATURES)):
        feat_np[:, col] = part.reshape(-1)
    feat = jnp.asarray(feat_np, BF16)
    hs_np = np.zeros((dh2 + LANES, LANES), np.float32)
    hs_np[:dh, :dh] = 1.0
    hs_np[dh:dh2, dh:] = 1.0
    hs_np[dh2:dh2 + N_BIAS_FEATURES, :] = 1.0
    hs = jnp.asarray(hs_np, BF16)

    def page_spec(pi):
        return pl.BlockSpec((None, None, page, nh, dh2),
                            lambda bb, g, pt: (layer, pt[bb, g * n_pg + pi], 0, 0, 0))

    tok = pl.BlockSpec((None, nh, dh2), lambda bb, g, pt: (bb, 0, 0))

    def const(shape):
        zeros = (0,) * len(shape)
        return pl.BlockSpec(shape, lambda bb, g, pt: zeros)

    grid_spec = pltpu.PrefetchScalarGridSpec(
        num_scalar_prefetch=1,
        grid=(b, n_pages // n_pg),
        in_specs=[page_spec(pi) for pi in range(n_pg)] * 2
                 + [const(feat.shape), const(hs.shape), tok, tok, tok, const((1, dh2)), const((nh, LANES))]
                 + [const((1, dh))] * 4,
        out_specs=tok,
        scratch_shapes=[pltpu.VMEM((n_pg, page * nh, LANES), F32), pltpu.VMEM((2, nh, dh2), F32),
                        pltpu.VMEM((nh, LANES), F32), pltpu.VMEM((nh, LANES), F32)],
    )
    est = 2 * 2 * n_pg * page * nh * dh2 * 4 + n_pg * page * nh * LANES * 4 + page * nh * LANES * 4 * 12
    return pl.pallas_call(
        functools.partial(_dec_attn_kernel, n_pg=n_pg, page=page, past_len=n_pages * page, dh=dh,
                          lam_init=lam_init),
        out_shape=jax.ShapeDtypeStruct((b, nh, dh2), F32),
        grid_spec=grid_spec,
        compiler_params=pltpu.CompilerParams(
            dimension_semantics=("arbitrary", "arbitrary"), vmem_limit_bytes=_vmem_limit(est + (8 << 20))),
        name="decode_diff_attn",
    )(page_table, *([cache_k] * n_pg), *([cache_v] * n_pg), feat, hs, q, kn, vn, sub_g.reshape(1, dh2), slopes,
      *[a.reshape(1, dh) for a in lams])


FFN_CHUNK = MXU_TILE
ROW_TILE = 512
FLASH_Q_TILE = 1024
FLASH_K_TILE = 512
DECODE_PAGES_PER_STEP = 8
DECODE_PAGES_PER_UPDATE = 2
POSITION_CHAINS = 8


def _prep_ffn_weights(w_gu, w_d):
    assert w_d.shape[0] % FFN_CHUNK == 0
    return w_gu.astype(BF16), w_d.astype(BF16)


def kernel(x_prompt, x_sample, state_ret, cache_k, cache_v, page_table, norm_g, ffn_w_gu, ffn_w_d, final_g,
           ret_w_in, ret_w_out, ret_gn_g, ret_gn_b, att_w_in, att_w_out, att_lam_q1, att_lam_k1, att_lam_q2,
           att_lam_k2, att_subln_g):
    depth = norm_g.shape[0]
    batch, seq, d_model = x_prompt.shape
    dec_batch, dec_seq, _ = x_sample.shape
    assert batch == 1 and dec_seq == 1
    ret_dk = d_model // RET_HEADS
    ret_dv = 2 * d_model // RET_HEADS
    att_dh2 = d_model // ATT_HEADS

    yp = x_prompt.reshape(seq, d_model)
    ys = x_sample.reshape(dec_batch, d_model)
    ret_p, ret_s, kp_rows, vp_rows, ks_rows, vs_rows = [], [], [], [], [], []
    for i in range(depth):
        r = i // N_MIXERS
        last = i == depth - 1
        wgu0, wd0 = _prep_ffn_weights(ffn_w_gu[i, 0], ffn_w_d[i, 0])
        wgu1, wd1 = _prep_ffn_weights(ffn_w_gu[i, 1], ffn_w_d[i, 1])
        yp = _ffn(yp, norm_g[i, 0], wgu0, wd0, final_g, tm=ROW_TILE, final=False)
        ys = _ffn(ys, norm_g[i, 0], wgu0, wd0, final_g, tm=ROW_TILE, final=False)
        if i % N_MIXERS == 0:
            w_in = ret_w_in[r].astype(BF16)
            w_out = ret_w_out[r].astype(BF16)
            k_scale = ret_dk ** -0.5
            qkvg_p = _ret_proj(yp, norm_g[i, 1], w_in, tm=ROW_TILE, out_dtype=BF16, k_scale=k_scale)
            qkvg_s = _ret_proj(ys, norm_g[i, 1], w_in, tm=ROW_TILE, out_dtype=F32, k_scale=k_scale)
            op, sp = _ret_scan(qkvg_p, ret_gn_g[r], ret_gn_b[r], dk=ret_dk, dv=ret_dv)
            osm, ss = _ret_step(qkvg_s, state_ret, r, ret_gn_g[r], ret_gn_b[r], dk=ret_dk, dv=ret_dv)
            ret_p.append(sp.reshape(1, RET_HEADS, ret_dk, ret_dv))
            ret_s.append(ss)
        else:
            w_in = att_w_in[r].astype(BF16)
            w_out = att_w_out[r].astype(BF16)
            lam_init = 0.8 - 0.6 * math.exp(-0.3 * i)
            lams = (att_lam_q1[r], att_lam_k1[r], att_lam_q2[r], att_lam_k2[r])
            kp, vp, qt, kb, vt = _att_proj(yp, norm_g[i, 1], w_in, tm=FLASH_K_TILE, head_major=True)
            qs, ksm, vsm = _att_proj(ys, norm_g[i, 1], w_in, tm=ROW_TILE, head_major=False)
            op = _flash(qt, kb, vt, lams, att_subln_g[r], tq=FLASH_Q_TILE, lam_init=lam_init)
            hs = (dec_batch, ATT_HEADS, att_dh2)
            osm = _dec_attn(qs.reshape(hs), ksm.reshape(hs), vsm.reshape(hs), cache_k, cache_v, r, page_table,
                            lams, att_subln_g[r], lam_init=lam_init, n_pg=DECODE_PAGES_PER_STEP)
            osm = osm.reshape(dec_batch, d_model)
            kp_rows.append(kp.reshape(1, seq, ATT_HEADS, att_dh2))
            vp_rows.append(vp.reshape(1, seq, ATT_HEADS, att_dh2))
            ks_rows.append(ksm.reshape(dec_batch, 1, ATT_HEADS, att_dh2))
            vs_rows.append(vsm.reshape(dec_batch, 1, ATT_HEADS, att_dh2))
        yp = _ffn(yp, norm_g[i, 2], wgu1, wd1, final_g, tm=ROW_TILE, final=last, mix=op, w_out=w_out)
        ys = _ffn(ys, norm_g[i, 2], wgu1, wd1, final_g, tm=ROW_TILE, final=last, mix=osm, w_out=w_out)
    return (yp.reshape(1, seq, d_model), ys.reshape(dec_batch, 1, d_model),
            jnp.stack(ret_p), jnp.stack(kp_rows), jnp.stack(vp_rows),
            jnp.stack(ret_s), jnp.stack(ks_rows), jnp.stack(vs_rows))
```

```python
import functools
import math

import numpy as np
import jax
import jax.numpy as jnp
from jax import lax
from jax.experimental import pallas as pl
from jax.experimental.pallas import tpu as pltpu

F32 = jnp.float32
BF16 = jnp.bfloat16

NORM_EPS = 1e-6
GN_EPS = 1e-5
NEG_BIG = -1e30
LOG2E = 1.4426950408889634
N_MIXERS = 2

RET_HEADS = 8
RET_CHUNK = 128
RET_CHUNKS_PER_STEP = 4
RET_SEQS_PER_STEP = 4
ATT_HEADS = 8

V7X_VMEM_BYTES = 64 * 1024 * 1024
MXU_TILE = 256
LANES = 128
BF16_SUBLANES = 16
F32_SUBLANES = 8
N_BIAS_FEATURES = 3


def _vmem_limit(nbytes):
    return int(min(V7X_VMEM_BYTES - 8 * 1024 * 1024, max(32 * 1024 * 1024, nbytes)))


def _rms(x, g):
    return x * lax.rsqrt(jnp.mean(x * x, axis=-1, keepdims=True) + NORM_EPS) * g


def _resident(shape):
    zeros = (0,) * len(shape)
    return pl.BlockSpec(shape, lambda *_: zeros, pipeline_mode=pl.Buffered(1))


def _ffn_kernel(x_ref, *refs, d_ff, ck, final, mixed):
    if mixed:
        mix_ref, wo_ref, wgu_ref, wd_ref, g_ref, fg_ref, o_ref, xn_ref, acc_ref = refs
        x = x_ref[...] + jnp.dot(mix_ref[...].astype(BF16), wo_ref[...], preferred_element_type=F32)
    else:
        wgu_ref, wd_ref, g_ref, fg_ref, o_ref, xn_ref, acc_ref = refs
        x = x_ref[...]
    xn_ref[...] = _rms(x, g_ref[...]).astype(BF16)
    acc_ref[...] = jnp.zeros_like(acc_ref)
    for c0 in range(0, d_ff, ck):
        xn = xn_ref[...]
        gate = jnp.dot(xn, wgu_ref[:, c0:c0 + ck], preferred_element_type=F32)
        up = jnp.dot(xn, wgu_ref[:, d_ff + c0:d_ff + c0 + ck], preferred_element_type=F32)
        h = (jax.nn.silu(gate) * up).astype(BF16)
        acc_ref[...] += jnp.dot(h, wd_ref[c0:c0 + ck, :], preferred_element_type=F32)
    y = x + 0.5 * acc_ref[...]
    if final:
        y = _rms(y, fg_ref[...])
    o_ref[...] = y


def _ffn(x, g, wgu_all, wd_all, which, final_g, *, tm, final, mix=None, w_out=None):
    m, d = x.shape
    d_ff = wd_all.shape[2]
    ck = FFN_CHUNK
    assert d_ff % ck == 0
    tm = min(tm, m)
    mixed = mix is not None
    row = pl.BlockSpec((tm, d), lambda i: (i, 0))

    def layer_weights(shape):
        return pl.BlockSpec((None, None) + shape, lambda i: which + (0, 0), pipeline_mode=pl.Buffered(1))

    est = (d * 2 * d_ff + d_ff * d) * 2 + tm * d * (4 * 4 + 2 + 4) + tm * 2 * ck * 4 * 3
    mix_specs, mix_args = [], []
    if mixed:
        kdim = mix.shape[1]
        mix_specs = [pl.BlockSpec((tm, kdim), lambda i: (i, 0)), _resident(w_out.shape)]
        mix_args = [mix, w_out]
        est += w_out.size * 2 + tm * kdim * 4 * 2 + tm * d * 4
    return pl.pallas_call(
        functools.partial(_ffn_kernel, d_ff=d_ff, ck=ck, final=final, mixed=mixed),
        out_shape=jax.ShapeDtypeStruct((m, d), F32),
        grid=(m // tm,),
        in_specs=[row] + mix_specs + [layer_weights((d, 2 * d_ff)), layer_weights((d_ff, d)),
                                      _resident((1, d)), _resident((1, d))],
        out_specs=row,
        scratch_shapes=[pltpu.VMEM((tm, d), BF16), pltpu.VMEM((tm, d), F32)],
        compiler_params=pltpu.CompilerParams(
            dimension_semantics=("arbitrary",), vmem_limit_bytes=_vmem_limit(est + (8 << 20))),
        name="ffn",
    )(x, *mix_args, wgu_all, wd_all, g.reshape(1, d), final_g.reshape(1, d))


def _ret_proj_kernel(x_ref, g_ref, w_ref, o_ref, *, d, k_scale):
    xn = _rms(x_ref[...], g_ref[...]).astype(BF16)
    n = w_ref.shape[1]
    for c in range(n // d):
        r = jnp.dot(xn, w_ref[:, c * d:(c + 1) * d], preferred_element_type=F32)
        if c == 1:
            r = r * k_scale
        o_ref[:, c * d:(c + 1) * d] = r.astype(o_ref.dtype)


def _ret_proj(x, g, w, *, tm, out_dtype, k_scale):
    m, d = x.shape
    n = w.shape[1]
    tm = min(tm, m)
    est = w.size * 2 + tm * d * 4 * 2 + tm * n * 4 * 2 + tm * d * 4 * 3
    return pl.pallas_call(
        functools.partial(_ret_proj_kernel, d=d, k_scale=k_scale),
        out_shape=jax.ShapeDtypeStruct((m, n), out_dtype),
        grid=(m // tm,),
        in_specs=[pl.BlockSpec((tm, d), lambda i: (i, 0)), _resident((1, d)), _resident(w.shape)],
        out_specs=pl.BlockSpec((tm, n), lambda i: (i, 0)),
        compiler_params=pltpu.CompilerParams(
            dimension_semantics=("arbitrary",), vmem_limit_bytes=_vmem_limit(est + (8 << 20))),
        name="ret_proj",
    )(x, g.reshape(1, d), w)


def _ret_consts(chunk):
    lg = np.log1p(-np.exp2(-5.0 - np.arange(RET_HEADS, dtype=np.float64)))
    pos = np.arange(chunk, dtype=np.float64)
    n = pos[:, None] - pos[None, :]
    decay = np.where(n[None] >= 0, np.exp(lg[:, None, None] * np.maximum(n, 0.0)[None]), 0.0)
    q_dec = np.exp(lg[:, None] * (pos[None, :] + 1.0))
    k_dec = np.exp(lg[:, None] * (chunk - 1.0 - pos)[None, :])
    s_dec = np.exp(lg * chunk)
    return decay, q_dec, k_dec, s_dec


def _group_norm_gate(o, gate, gn_g, gn_b):
    mu = jnp.mean(o, axis=-1, keepdims=True)
    var = jnp.mean(jnp.square(o - mu), axis=-1, keepdims=True)
    on = (o - mu) * lax.rsqrt(var + GN_EPS)
    on = on * gn_g + gn_b
    return on * jax.nn.silu(gate)


def _ret_scan_kernel(q_ref, k_ref, v_ref, gt_ref, dec_ref, qd_ref, kd_ref, gng_ref, gnb_ref,
                     o_ref, s_out_ref, s_ref, *, dk, dv, s_dec):
    t = pl.program_id(0)

    @pl.when(t == 0)
    def _():
        s_ref[...] = jnp.zeros_like(s_ref)

    c = dec_ref.shape[1]
    for r0 in range(0, q_ref.shape[0], c):
        rows = slice(r0, r0 + c)
        for h in range(RET_HEADS):
            qh = q_ref[rows, h * dk:(h + 1) * dk]
            kh = k_ref[rows, h * dk:(h + 1) * dk]
            vh = v_ref[rows, h * dv:(h + 1) * dv]
            sc = lax.dot_general(qh, kh, (((1,), (1,)), ((), ())), preferred_element_type=F32) * dec_ref[h]
            o = jnp.dot(sc.astype(BF16), vh, preferred_element_type=F32)
            s_old = s_ref[h]
            qd = (qh.astype(F32) * qd_ref[h]).astype(BF16)
            o = o + jnp.dot(qd, s_old.astype(BF16), preferred_element_type=F32)
            kd = (kh.astype(F32) * kd_ref[h]).astype(BF16)
            s_ref[h] = s_dec[h] * s_old + lax.dot_general(kd, vh, (((0,), (0,)), ((), ())),
                                                          preferred_element_type=F32)
            gate = gt_ref[rows, h * dv:(h + 1) * dv].astype(F32)
            res = _group_norm_gate(o, gate, gng_ref[:, h * dv:(h + 1) * dv], gnb_ref[:, h * dv:(h + 1) * dv])
            o_ref[rows, h * dv:(h + 1) * dv] = res.astype(o_ref.dtype)

    @pl.when(t == pl.num_programs(0) - 1)
    def _():
        s_out_ref[...] = s_ref[...]


def _ret_scan(qkvg, gn_g, gn_b, *, dk, dv):
    t_len = qkvg.shape[0]
    c = RET_CHUNK if t_len % RET_CHUNK == 0 else t_len
    hk, hv = RET_HEADS * dk, RET_HEADS * dv
    decay, q_dec, k_dec, s_dec = _ret_consts(c)
    dec = jnp.asarray(decay, F32)
    qd = jnp.asarray(np.broadcast_to(q_dec[:, :, None], (RET_HEADS, c, dk)), F32)
    kd = jnp.asarray(np.broadcast_to(k_dec[:, :, None], (RET_HEADS, c, dk)), F32)
    rows = c * RET_CHUNKS_PER_STEP if t_len % (c * RET_CHUNKS_PER_STEP) == 0 else c
    return pl.pallas_call(
        functools.partial(_ret_scan_kernel, dk=dk, dv=dv, s_dec=tuple(float(s) for s in s_dec)),
        out_shape=(jax.ShapeDtypeStruct((t_len, hv), BF16),
                   jax.ShapeDtypeStruct((RET_HEADS, dk, dv), F32)),
        grid=(t_len // rows,),
        in_specs=[
            pl.BlockSpec((rows, hk), lambda t: (t, 0)),
            pl.BlockSpec((rows, hk), lambda t: (t, 1)),
            pl.BlockSpec((rows, hv), lambda t: (t, 1)),
            pl.BlockSpec((rows, hv), lambda t: (t, 2)),
            _resident(dec.shape), _resident(qd.shape), _resident(kd.shape),
            _resident((1, hv)), _resident((1, hv)),
        ],
        out_specs=(pl.BlockSpec((rows, hv), lambda t: (t, 0)),
                   pl.BlockSpec((RET_HEADS, dk, dv), lambda t: (0, 0, 0))),
        scratch_shapes=[pltpu.VMEM((RET_HEADS, dk, dv), F32)],
        compiler_params=pltpu.CompilerParams(dimension_semantics=("arbitrary",)),
        name="ret_scan",
    )(qkvg, qkvg, qkvg, qkvg, dec, qd, kd, gn_g.reshape(1, hv), gn_b.reshape(1, hv))


def _ret_step_kernel(x_ref, s_ref, gng_ref, gnb_ref, o_ref, s_out_ref, *, dk, dv, gamma):
    nh = RET_HEADS
    per_head = dv // LANES

    def head_row(x, base, h):
        r0 = base + per_head * h
        return jnp.concatenate([x[r0 + j:r0 + j + 1] for j in range(per_head)], axis=1)

    for b in range(x_ref.shape[0]):
        x = x_ref[b]
        qk_cols = jnp.transpose(jnp.concatenate([x[:2 * nh], jnp.zeros((dk - 2 * nh, dk), F32)], axis=0))
        for h in range(nh):
            q = x[h:h + 1]
            k = x[nh + h:nh + h + 1]
            v = head_row(x, 2 * nh, h)
            gate = head_row(x, 2 * nh + per_head * nh, h)
            q_col = qk_cols[:, h:h + 1]
            k_col = qk_cols[:, nh + h:nh + h + 1]
            s_old = s_ref[b, h]
            qk = jnp.sum(q * k, axis=1, keepdims=True)
            o = qk * v + jnp.sum((q_col * gamma[h]) * s_old, axis=0, keepdims=True)
            s_out_ref[b, h] = gamma[h] * s_old + k_col * v
            res = _group_norm_gate(o, gate, gng_ref[:, h * dv:(h + 1) * dv], gnb_ref[:, h * dv:(h + 1) * dv])
            o_ref[b, :, h * dv:(h + 1) * dv] = res


def _ret_step(qkvg, state, layer, gn_g, gn_b, *, dk, dv):
    b, n = qkvg.shape
    hv = RET_HEADS * dv
    assert dk == LANES and dv % LANES == 0
    nb = RET_SEQS_PER_STEP if b % RET_SEQS_PER_STEP == 0 else 1
    gamma = tuple(float(1.0 - 2.0 ** (-5.0 - h)) for h in range(RET_HEADS))
    o, s_new = pl.pallas_call(
        functools.partial(_ret_step_kernel, dk=dk, dv=dv, gamma=gamma),
        out_shape=(jax.ShapeDtypeStruct((b, 1, hv), F32),
                   jax.ShapeDtypeStruct(state.shape[1:], F32)),
        grid=(b // nb,),
        in_specs=[
            pl.BlockSpec((nb, n // LANES, LANES), lambda i: (i, 0, 0)),
            pl.BlockSpec((None, nb, RET_HEADS, dk, dv), lambda i: (layer, i, 0, 0, 0)),
            _resident((1, hv)), _resident((1, hv)),
        ],
        out_specs=(pl.BlockSpec((nb, 1, hv), lambda i: (i, 0, 0)),
                   pl.BlockSpec((nb, RET_HEADS, dk, dv), lambda i: (i, 0, 0, 0))),
        compiler_params=pltpu.CompilerParams(dimension_semantics=("arbitrary",)),
        name="ret_step",
    )(qkvg.reshape(b, n // LANES, LANES), state, gn_g.reshape(1, hv), gn_b.reshape(1, hv))
    return o.reshape(b, hv), s_new


def _att_proj_kernel(x_ref, g_ref, w_ref, *out_refs, d, dh2, q_scale, head_major):
    xn = _rms(x_ref[...], g_ref[...]).astype(BF16)
    q = jnp.dot(xn, w_ref[:, 0:d], preferred_element_type=F32) * q_scale
    k = jnp.dot(xn, w_ref[:, d:2 * d], preferred_element_type=F32)
    v = jnp.dot(xn, w_ref[:, 2 * d:3 * d], preferred_element_type=F32)
    if head_major:
        k_ref, v_ref, qt_ref, kb_ref, vt_ref = out_refs
        k_ref[...] = k
        v_ref[...] = v
        for h in range(ATT_HEADS):
            qt_ref[h, 0] = jnp.transpose(q[:, h * dh2:(h + 1) * dh2]).astype(BF16)
            kb_ref[h] = k[:, h * dh2:(h + 1) * dh2].astype(BF16)
            vt_ref[h, 0] = jnp.transpose(v[:, h * dh2:(h + 1) * dh2]).astype(BF16)
    else:
        q_ref, k_ref, v_ref = out_refs
        q_ref[...] = q
        k_ref[...] = k
        v_ref[...] = v


def _att_proj(x, g, w, *, tm, head_major):
    m, d = x.shape
    dh2 = d // ATT_HEADS
    tm = min(tm, m)
    row = pl.BlockSpec((tm, d), lambda i: (i, 0))
    if head_major:
        hm = pl.BlockSpec((ATT_HEADS, tm, dh2), lambda i: (0, i, 0))
        hm_t = pl.BlockSpec((ATT_HEADS, 1, dh2, tm), lambda i: (0, i, 0, 0))
        t_shape = jax.ShapeDtypeStruct((ATT_HEADS, m // tm, dh2, tm), BF16)
        out_shape = (jax.ShapeDtypeStruct((m, d), F32),) * 2 + (
            t_shape, jax.ShapeDtypeStruct((ATT_HEADS, m, dh2), BF16), t_shape)
        out_specs = (row, row, hm_t, hm, hm_t)
    else:
        out_shape = (jax.ShapeDtypeStruct((m, d), F32),) * 3
        out_specs = (row, row, row)
    est = w.size * 2 + tm * d * 4 * 12
    return pl.pallas_call(
        functools.partial(_att_proj_kernel, d=d, dh2=dh2, q_scale=(dh2 // 2) ** -0.5 * LOG2E,
                          head_major=head_major),
        out_shape=out_shape,
        grid=(m // tm,),
        in_specs=[row, _resident((1, d)), _resident(w.shape)],
        out_specs=out_specs,
        compiler_params=pltpu.CompilerParams(
            dimension_semantics=("arbitrary",), vmem_limit_bytes=_vmem_limit(est + (8 << 20))),
        name="att_proj",
    )(x, g.reshape(1, d), w)


def _alibi_slopes():
    return np.exp2(-8.0 * (np.arange(ATT_HEADS, dtype=np.float64) + 1.0) / ATT_HEADS)


def _lambda(lq1_ref, lk1_ref, lq2_ref, lk2_ref, lam_init):
    a = jnp.sum(lq1_ref[...] * lk1_ref[...], axis=1, keepdims=True)
    b = jnp.sum(lq2_ref[...] * lk2_ref[...], axis=1, keepdims=True)
    return jnp.exp(a) - jnp.exp(b) + lam_init


def _sub_norm(o, sub_g, lam_init):
    return _rms(o, sub_g) * (1.0 - lam_init)


def _flash_kernel(mask_ref, k_ref, vt_ref, qt_ref, kfeat_ref, subg_ref, slope_ref, lq1_ref, lk1_ref, lq2_ref, lk2_ref,
                  o_ref, qz_ref, sa_ref, sb_ref, acc_ref, m_ref, *, tq, tk, dh, lam_init):
    i = pl.program_id(1)
    dv = 2 * dh
    dh2 = 2 * dh
    ratio = tq // tk
    slope2 = slope_ref[:, :1] * LOG2E

    qt = jnp.concatenate([qt_ref[b] for b in range(qt_ref.shape[0])], axis=1)
    feat = lax.broadcasted_iota(jnp.int32, qt.shape, 0)
    zero = jnp.zeros_like(qt)
    qz_ref[:dh2, :tq] = jnp.where(feat < dh, qt, zero)
    qz_ref[:dh2, tq:] = jnp.where(feat >= dh, qt, zero)
    bias_row = lax.broadcasted_iota(jnp.int32, (LANES, 2 * tq), 0) < N_BIAS_FEATURES
    qz_ref[dh2:, :] = jnp.where(bias_row, 1.0, 0.0).astype(BF16)
    m_ref[...] = jnp.full_like(m_ref, NEG_BIG)
    acc_ref[...] = jnp.zeros_like(acc_ref)
    ones = jnp.ones((BF16_SUBLANES, tk), BF16)
    kfeat = kfeat_ref[...]

    def scores(j, s_ref):
        kt = k_ref[pl.ds(pl.multiple_of(j * tk, tk), tk), :]
        s_ref[...] = jnp.dot(jnp.concatenate([kt, kfeat], axis=1), qz_ref[...],
                             preferred_element_type=F32)

    def consume(j, s_ref, diag_tile=None):
        va = jnp.concatenate([vt_ref[j], ones], axis=0)
        blk = jnp.full((1, 1), j * tk - i * tq, jnp.int32).astype(F32) * slope2
        s = s_ref[...]
        if diag_tile is not None:
            mask = mask_ref[diag_tile]
            s = s + jnp.concatenate([mask, mask], axis=1)
        m_old = m_ref[0:1]
        m_new = jnp.maximum(m_old, jnp.max(s, axis=0, keepdims=True) + blk)
        p = jnp.exp2(s - (m_new - blk))
        alpha = jnp.exp2(m_old - m_new)
        acc_ref[...] = alpha * acc_ref[...] + jnp.dot(va, p.astype(BF16), preferred_element_type=F32)
        m_ref[0:1] = m_new

    scores(0, sa_ref)

    def pair(t):
        a = 2 * t
        scores(a + 1, sb_ref)
        consume(a, sa_ref)
        scores(a + 2, sa_ref)
        consume(a + 1, sb_ref)

    def two_pairs(t, carry):
        pair(2 * t)
        pair(2 * t + 1)
        return carry

    first_diag = ratio * i
    n_pairs = first_diag // 2
    lax.fori_loop(0, n_pairs // 2, two_pairs, 0)

    @pl.when(n_pairs % 2 == 1)
    def _():
        pair(n_pairs - 1)

    for d in range(0, ratio, 2):
        scores(first_diag + d + 1, sb_ref)
        consume(first_diag + d, sa_ref, diag_tile=d)
        if d + 2 < ratio:
            scores(first_diag + d + 2, sa_ref)
        consume(first_diag + d + 1, sb_ref, diag_tile=d + 1)

    lam = _lambda(lq1_ref, lk1_ref, lq2_ref, lk2_ref, lam_init)
    a1 = acc_ref[:, :tq]
    a2 = acc_ref[:, tq:]
    o = a1[:dv] / a1[dv:dv + 1] - lam * (a2[:dv] / a2[dv:dv + 1])
    on = o * lax.rsqrt(jnp.mean(o * o, axis=0, keepdims=True) + NORM_EPS) * subg_ref[...] * (1.0 - lam_init)
    o_ref[...] = jnp.transpose(on).astype(o_ref.dtype)


def _bf16_split(x, n_parts):
    parts = []
    rest = np.asarray(x, np.float32)
    for _ in range(n_parts):
        part = rest.astype(BF16).astype(np.float32)
        parts.append(part)
        rest = rest - part
    return parts


def _flash(qt, kb, vt, lams, sub_g, *, tq, lam_init):
    h, n_kt, dh2, tk = vt.shape
    t_len = kb.shape[1]
    dh = dh2 // 2
    tq = max(min(tq, t_len), 2 * tk)
    ratio = tq // tk
    assert tq % tk == 0 and ratio % 2 == 0 and t_len % tq == 0
    slopes_np = _alibi_slopes()
    slopes = jnp.asarray(np.broadcast_to(slopes_np[:, None, None], (h, 1, LANES)), F32)
    local = (np.arange(tk, dtype=np.float64)[None, :] * (slopes_np * LOG2E)[:, None]).astype(np.float32)
    kfeat_np = np.zeros((h, tk, LANES), np.float32)
    for col, part in enumerate(_bf16_split(local, N_BIAS_FEATURES)):
        kfeat_np[:, :, col] = part
    kfeat = jnp.asarray(kfeat_np, BF16)
    key = np.arange(tk)[None, :, None] + tk * np.arange(ratio)[:, None, None]
    masks = jnp.asarray(np.where(key <= np.arange(tq)[None, None, :], 0.0, NEG_BIG), F32)
    lam_specs = [_resident((1, dh))] * 4
    est = (2 * 2 * t_len * dh2 * 2 + ratio * tq * tk * 4 + 2 * tq * tk * 4 * 5
           + 2 * tq * 2 * dh2 * 4 * 3 + 4 * tq * dh2 * 2 * 2)
    return pl.pallas_call(
        functools.partial(_flash_kernel, tq=tq, tk=tk, dh=dh, lam_init=lam_init),
        out_shape=jax.ShapeDtypeStruct((t_len, h * dh2), BF16),
        grid=(h, t_len // tq),
        in_specs=[
            _resident(masks.shape),
            pl.BlockSpec((None, t_len, dh2), lambda hh, i: (hh, 0, 0)),
            pl.BlockSpec((None, n_kt, dh2, tk), lambda hh, i: (hh, 0, 0, 0)),
            pl.BlockSpec((None, ratio, dh2, tk), lambda hh, i: (hh, i, 0, 0)),
            pl.BlockSpec((None, tk, LANES), lambda hh, i: (hh, 0, 0)),
            _resident((dh2, 1)),
            pl.BlockSpec((None, 1, LANES), lambda hh, i: (hh, 0, 0)),
        ] + lam_specs,
        out_specs=pl.BlockSpec((tq, dh2), lambda hh, i: (i, hh)),
        scratch_shapes=[pltpu.VMEM((dh2 + LANES, 2 * tq), BF16),
                        pltpu.VMEM((tk, 2 * tq), F32), pltpu.VMEM((tk, 2 * tq), F32),
                        pltpu.VMEM((dh2 + BF16_SUBLANES, 2 * tq), F32), pltpu.VMEM((F32_SUBLANES, 2 * tq), F32)],
        compiler_params=pltpu.CompilerParams(
            dimension_semantics=("arbitrary", "arbitrary"), vmem_limit_bytes=_vmem_limit(est + (8 << 20))),
        name="flash_diff_attn",
    )(masks, kb, vt, qt, kfeat, sub_g.reshape(dh2, 1), slopes, *[a.reshape(1, dh) for a in lams])


def _dec_attn_kernel(pt_ref, *rest, n_pg, page, past_len, dh, lam_init):
    del pt_ref
    k_refs = rest[:n_pg]
    v_refs = rest[n_pg:2 * n_pg]
    (feat_ref, hs_ref, q_ref, kn_ref, vn_ref, subg_ref, slope_ref, lq1_ref, lk1_ref, lq2_ref, lk2_ref,
     o_ref, sc_ref, acc_ref, m_ref, l_ref) = rest[2 * n_pg:]
    g = pl.program_id(1)
    dh2 = 2 * dh
    nh = q_ref.shape[0]

    @pl.when(g == 0)
    def _():
        m_ref[...] = jnp.full_like(m_ref, NEG_BIG)
        l_ref[...] = jnp.zeros_like(l_ref)
        acc_ref[...] = jnp.zeros_like(acc_ref)

    q = q_ref[...]
    slope2 = slope_ref[...] * LOG2E
    feat = feat_ref[...]
    hs = hs_ref[...]

    def over_positions(x, op):
        n = x.shape[0]
        if n % POSITION_CHAINS == 0 and n > POSITION_CHAINS:
            x = op(x.reshape(POSITION_CHAINS, n // POSITION_CHAINS, nh, x.shape[-1]), axis=1)
        return op(x, axis=0)

    def update(ss, offs, vs):
        m_old = m_ref[...]
        m_new = m_old
        for s, off in zip(ss, offs):
            m_new = jnp.maximum(m_new, over_positions(s, jnp.max) + off)
        l_add = jnp.zeros_like(m_old)
        a_add = jnp.zeros_like(m_old)
        b_add = jnp.zeros_like(m_old)
        for s, off, v in zip(ss, offs, vs):
            p = jnp.exp2(s - (m_new - off)[None])
            l_add = l_add + over_positions(p, jnp.sum)
            a_add = a_add + over_positions(p * v, jnp.sum)
            b_add = b_add + over_positions(p * pltpu.roll(v, dh, 2), jnp.sum)
        alpha = jnp.exp2(m_old - m_new)
        l_ref[...] = alpha * l_ref[...] + l_add
        acc_ref[0] = alpha * acc_ref[0] + a_add
        acc_ref[1] = alpha * acc_ref[1] + b_add
        m_ref[...] = m_new

    offs = []
    for pi in range(n_pg):
        kp = k_refs[pi][...]
        prod = (kp * q[None]).astype(BF16).reshape(page * nh, dh2)
        sc_ref[pi] = jnp.dot(jnp.concatenate([prod, feat], axis=1), hs, preferred_element_type=F32)
        first_pos = (g * n_pg + pi) * page - past_len
        offs.append(jnp.full((1, 1), first_pos, jnp.int32).astype(F32) * slope2)
    for g0 in range(0, n_pg, DECODE_PAGES_PER_UPDATE):
        pages = range(g0, g0 + DECODE_PAGES_PER_UPDATE)
        update([sc_ref[pi].reshape(page, nh, LANES) for pi in pages], [offs[pi] for pi in pages],
               [v_refs[pi][...] for pi in pages])

    @pl.when(g == pl.num_programs(1) - 1)
    def _():
        kn = kn_ref[...]
        vn = vn_ref[...]
        s_new = jnp.dot((kn * q).astype(BF16), hs[:dh2], preferred_element_type=F32)
        update([s_new[None]], [jnp.zeros_like(slope2)], [vn[None]])
        lam = _lambda(lq1_ref, lk1_ref, lq2_ref, lk2_ref, lam_init)
        first_half = lax.broadcasted_iota(jnp.int32, (nh, LANES), 1) < dh

        def swap(x):
            return pltpu.roll(x, dh, 1)

        l = l_ref[...]
        l1 = jnp.where(first_half, l, swap(l))
        l2 = jnp.where(first_half, swap(l), l)
        a = acc_ref[0]
        b = acc_ref[1]
        o1 = jnp.where(first_half, a, swap(b))
        o2 = jnp.where(first_half, swap(b), a)
        o = o1 / l1 - lam * (o2 / l2)
        o_ref[...] = _sub_norm(o, subg_ref[...], lam_init)


def _dec_attn(q, kn, vn, cache_k, cache_v, layer, page_table, lams, sub_g, *, lam_init, n_pg):
    b, nh, dh2 = q.shape
    dh = dh2 // 2
    n_pages = page_table.shape[1]
    page = cache_k.shape[2]
    while n_pages % n_pg:
        n_pg //= 2
    slopes = jnp.asarray(np.broadcast_to(_alibi_slopes()[:, None], (nh, LANES)), F32)
    local = (np.arange(page, dtype=np.float64)[:, None] * (_alibi_slopes() * LOG2E)[None, :]).astype(np.float32)
    feat_np = np.zeros((page * nh, LANES), np.float32)
    for col, part in enumerate(_bf16_split(local, N_BIAS_FEATURES)):
        feat_np[:, col] = part.reshape(-1)
    feat = jnp.asarray(feat_np, BF16)
    hs_np = np.zeros((dh2 + LANES, LANES), np.float32)
    hs_np[:dh, :dh] = 1.0
    hs_np[dh:dh2, dh:] = 1.0
    hs_np[dh2:dh2 + N_BIAS_FEATURES, :] = 1.0
    hs = jnp.asarray(hs_np, BF16)

    def page_spec(pi):
        return pl.BlockSpec((None, None, page, nh, dh2),
                            lambda bb, g, pt: (layer, pt[bb, g * n_pg + pi], 0, 0, 0))

    tok = pl.BlockSpec((None, nh, dh2), lambda bb, g, pt: (bb, 0, 0))

    def const(shape):
        zeros = (0,) * len(shape)
        return pl.BlockSpec(shape, lambda bb, g, pt: zeros)

    grid_spec = pltpu.PrefetchScalarGridSpec(
        num_scalar_prefetch=1,
        grid=(b, n_pages // n_pg),
        in_specs=[page_spec(pi) for pi in range(n_pg)] * 2
                 + [const(feat.shape), const(hs.shape), tok, tok, tok, const((1, dh2)), const((nh, LANES))]
                 + [const((1, dh))] * 4,
        out_specs=tok,
        scratch_shapes=[pltpu.VMEM((n_pg, page * nh, LANES), F32), pltpu.VMEM((2, nh, dh2), F32),
                        pltpu.VMEM((nh, LANES), F32), pltpu.VMEM((nh, LANES), F32)],
    )
    est = 2 * 2 * n_pg * page * nh * dh2 * 4 + n_pg * page * nh * LANES * 4 + page * nh * LANES * 4 * 12
    return pl.pallas_call(
        functools.partial(_dec_attn_kernel, n_pg=n_pg, page=page, past_len=n_pages * page, dh=dh,
                          lam_init=lam_init),
        out_shape=jax.ShapeDtypeStruct((b, nh, dh2), F32),
        grid_spec=grid_spec,
        compiler_params=pltpu.CompilerParams(
            dimension_semantics=("arbitrary", "arbitrary"), vmem_limit_bytes=_vmem_limit(est + (8 << 20))),
        name="decode_diff_attn",
    )(page_table, *([cache_k] * n_pg), *([cache_v] * n_pg), feat, hs, q, kn, vn, sub_g.reshape(1, dh2), slopes,
      *[a.reshape(1, dh) for a in lams])


FFN_CHUNK = MXU_TILE
ROW_TILE = 512
FLASH_Q_TILE = 1024
FLASH_K_TILE = 512
DECODE_PAGES_PER_STEP = 8
DECODE_PAGES_PER_UPDATE = 2
POSITION_CHAINS = 8


def kernel(x_prompt, x_sample, state_ret, cache_k, cache_v, page_table, norm_g, ffn_w_gu, ffn_w_d, final_g,
           ret_w_in, ret_w_out, ret_gn_g, ret_gn_b, att_w_in, att_w_out, att_lam_q1, att_lam_k1, att_lam_q2,
           att_lam_k2, att_subln_g):
    depth = norm_g.shape[0]
    batch, seq, d_model = x_prompt.shape
    dec_batch, dec_seq, _ = x_sample.shape
    assert batch == 1 and dec_seq == 1
    ret_dk = d_model // RET_HEADS
    ret_dv = 2 * d_model // RET_HEADS
    att_dh2 = d_model // ATT_HEADS

    yp = x_prompt.reshape(seq, d_model)
    ys = x_sample.reshape(dec_batch, d_model)
    ret_p, ret_s, kp_rows, vp_rows, ks_rows, vs_rows = [], [], [], [], [], []
    wgu_all = ffn_w_gu.astype(BF16)
    wd_all = ffn_w_d.astype(BF16)
    for i in range(depth):
        r = i // N_MIXERS
        last = i == depth - 1
        yp = _ffn(yp, norm_g[i, 0], wgu_all, wd_all, (i, 0), final_g, tm=ROW_TILE, final=False)
        ys = _ffn(ys, norm_g[i, 0], wgu_all, wd_all, (i, 0), final_g, tm=ROW_TILE, final=False)
        if i % N_MIXERS == 0:
            w_in = ret_w_in[r].astype(BF16)
            w_out = ret_w_out[r].astype(BF16)
            k_scale = ret_dk ** -0.5
            qkvg_p = _ret_proj(yp, norm_g[i, 1], w_in, tm=ROW_TILE, out_dtype=BF16, k_scale=k_scale)
            qkvg_s = _ret_proj(ys, norm_g[i, 1], w_in, tm=ROW_TILE, out_dtype=F32, k_scale=k_scale)
            op, sp = _ret_scan(qkvg_p, ret_gn_g[r], ret_gn_b[r], dk=ret_dk, dv=ret_dv)
            osm, ss = _ret_step(qkvg_s, state_ret, r, ret_gn_g[r], ret_gn_b[r], dk=ret_dk, dv=ret_dv)
            ret_p.append(sp.reshape(1, RET_HEADS, ret_dk, ret_dv))
            ret_s.append(ss)
        else:
            w_in = att_w_in[r].astype(BF16)
            w_out = att_w_out[r].astype(BF16)
            lam_init = 0.8 - 0.6 * math.exp(-0.3 * i)
            lams = (att_lam_q1[r], att_lam_k1[r], att_lam_q2[r], att_lam_k2[r])
            kp, vp, qt, kb, vt = _att_proj(yp, norm_g[i, 1], w_in, tm=FLASH_K_TILE, head_major=True)
            qs, ksm, vsm = _att_proj(ys, norm_g[i, 1], w_in, tm=ROW_TILE, head_major=False)
            op = _flash(qt, kb, vt, lams, att_subln_g[r], tq=FLASH_Q_TILE, lam_init=lam_init)
            hs = (dec_batch, ATT_HEADS, att_dh2)
            osm = _dec_attn(qs.reshape(hs), ksm.reshape(hs), vsm.reshape(hs), cache_k, cache_v, r, page_table,
                            lams, att_subln_g[r], lam_init=lam_init, n_pg=DECODE_PAGES_PER_STEP)
            osm = osm.reshape(dec_batch, d_model)
            kp_rows.append(kp.reshape(1, seq, ATT_HEADS, att_dh2))
            vp_rows.append(vp.reshape(1, seq, ATT_HEADS, att_dh2))
            ks_rows.append(ksm.reshape(dec_batch, 1, ATT_HEADS, att_dh2))
            vs_rows.append(vsm.reshape(dec_batch, 1, ATT_HEADS, att_dh2))
        yp = _ffn(yp, norm_g[i, 2], wgu_all, wd_all, (i, 1), final_g, tm=ROW_TILE, final=last, mix=op, w_out=w_out)
        ys = _ffn(ys, norm_g[i, 2], wgu_all, wd_all, (i, 1), final_g, tm=ROW_TILE, final=last, mix=osm, w_out=w_out)
    return (yp.reshape(1, seq, d_model), ys.reshape(dec_batch, 1, d_model),
            jnp.stack(ret_p), jnp.stack(kp_rows), jnp.stack(vp_rows),
            jnp.stack(ret_s), jnp.stack(ks_rows), jnp.stack(vs_rows))
```
